```python
import math
import jax, jax.numpy as jnp
from jax import lax
import numpy as np

D_MODEL = 4096
BATCH = 1
SEQ = 8192
DEPTH = 1
DEC_BATCH = 128
DEC_SEQ = 8
PAST_LEN = 2048
PAGE_SIZE = 128

HEAD_DIM = 128
D_ATTN = 3 * D_MODEL // 8
N_HEADS = D_ATTN // HEAD_DIM
D_SSM = D_MODEL - D_ATTN
SSM_GROUP = 16
N_SSM_GROUPS = D_SSM // SSM_GROUP
SSM_STATE = 64
DILATED_BRANCHES = ((128, 1), (512, 4), (2048, 16))
WIN_MAX = 2048
N_MEM = 256
MEM_HEADS = 4
MEM_HEAD_DIM = 128
D_FF = 4 * D_MODEL
ROPE_THETA = 10000.0
NORM_EPS = 1e-6
PROMPT_Q_BLOCK = 128
SSM_CHUNK = 128
DT_MIN = 1e-3
DT_MAX = 1e-1

kernel_name = "hybrid_dilated_attn_s5_decoder_step"


def rms_norm(x, g):
    xf = x.astype(jnp.float32)
    y = xf * lax.rsqrt(jnp.mean(xf * xf, axis=-1, keepdims=True) + NORM_EPS)
    return (y * g.astype(jnp.float32)).astype(x.dtype)


def rotary(x, pos):
    dh = x.shape[-1]
    inv = 1.0 / (ROPE_THETA ** (jnp.arange(0, dh, 2, dtype=jnp.float32) / dh))
    ang = pos.astype(jnp.float32)[:, None] * inv[None, :]
    cos = jnp.cos(ang)[None, :, None, :]
    sin = jnp.sin(ang)[None, :, None, :]
    xf = x.astype(jnp.float32)
    x1, x2 = xf[..., : dh // 2], xf[..., dh // 2:]
    return jnp.concatenate([x1 * cos - x2 * sin, x2 * cos + x1 * sin], axis=-1).astype(x.dtype)


def dilated_mixture_attention(q, k_all, v_all, q_pos, offset, block):
    b, nq, h, dh = q.shape
    n_keys = k_all.shape[1]
    nb = nq // block
    scale = 1.0 / math.sqrt(dh)
    qb = q.reshape(b, nb, block, h, dh).transpose(1, 0, 2, 3, 4)
    pb = q_pos.reshape(nb, block)

    def one_block(args):
        qi, pi = args
        outs, lses = [], []
        for window, dil in DILATED_BRANCHES:
            j = jnp.arange(window // dil + 1, dtype=jnp.int32)
            kpos = pi[:, None] - dil * j[None, :]
            valid = kpos >= 0
            idx = jnp.clip(kpos - offset, 0, n_keys - 1)
            kg = k_all[:, idx]
            vg = v_all[:, idx]
            s = jnp.einsum('bqhd,bqjhd->bhqj', qi, kg, preferred_element_type=jnp.float32) * scale
            s = jnp.where(valid[None, None], s, -jnp.inf)
            m = jnp.max(s, axis=-1, keepdims=True)
            p = jnp.exp(s - m)
            l = jnp.sum(p, axis=-1)
            o = jnp.einsum('bhqj,bqjhd->bqhd', p, vg.astype(jnp.float32))
            outs.append(o / l.transpose(0, 2, 1)[..., None])
            lses.append(m[..., 0] + jnp.log(l))
        wts = jax.nn.softmax(jnp.stack(lses), axis=0)
        wts = wts.transpose(0, 1, 3, 2)[..., None]
        return jnp.sum(wts * jnp.stack(outs), axis=0)

    ob = lax.map(one_block, (qb, pb))
    return ob.transpose(1, 0, 2, 3, 4).reshape(b, nq, h, dh).astype(q.dtype)


def s5_discretise(a_re, a_im, log_dt, b_re, b_im):
    a_re = a_re.astype(jnp.float32)
    a_im = a_im.astype(jnp.float32)
    dt = jnp.exp(log_dt.astype(jnp.float32))[:, None]
    mag = jnp.exp(a_re * dt)
    lam_re = mag * jnp.cos(a_im * dt)
    lam_im = mag * jnp.sin(a_im * dt)
    num_re = lam_re - 1.0
    num_im = lam_im
    den = a_re * a_re + a_im * a_im
    f_re = ((num_re * a_re + num_im * a_im) / den)[..., None]
    f_im = ((num_im * a_re - num_re * a_im) / den)[..., None]
    br = b_re.astype(jnp.float32)
    bi = b_im.astype(jnp.float32)
    return lam_re, lam_im, f_re * br - f_im * bi, f_re * bi + f_im * br


def _complex_affine_combine(e1, e2):
    a1r, a1i, b1r, b1i = e1
    a2r, a2i, b2r, b2i = e2
    return (a2r * a1r - a2i * a1i,
            a2r * a1i + a2i * a1r,
            a2r * b1r - a2i * b1i + b2r,
            a2r * b1i + a2i * b1r + b2i)


def s5_scan(u, lam_re, lam_im, bbar_re, bbar_im, c_re, c_im, h0_re, h0_im, chunk):
    b, s, g, ch = u.shape
    nc = s // chunk
    uc = u.reshape(b, nc, chunk, g, ch).transpose(1, 0, 2, 3, 4)
    cr = c_re.astype(jnp.float32)
    ci = c_im.astype(jnp.float32)

    def step(carry, u_blk):
        hr, hi = carry
        bu_re = jnp.einsum('btgc,gpc->btgp', u_blk, bbar_re)
        bu_im = jnp.einsum('btgc,gpc->btgp', u_blk, bbar_im)
        bu_re = bu_re.at[:, 0].add(lam_re * hr - lam_im * hi)
        bu_im = bu_im.at[:, 0].add(lam_re * hi + lam_im * hr)
        a_re = jnp.broadcast_to(lam_re, bu_re.shape)
        a_im = jnp.broadcast_to(lam_im, bu_im.shape)
        _, _, xr, xi = lax.associative_scan(_complex_affine_combine, (a_re, a_im, bu_re, bu_im), axis=1)
        y = jnp.einsum('btgp,gcp->btgc', xr, cr) - jnp.einsum('btgp,gcp->btgc', xi, ci)
        return (xr[:, -1], xi[:, -1]), y

    (hr, hi), ys = lax.scan(step, (h0_re.astype(jnp.float32), h0_im.astype(jnp.float32)), uc)
    return ys.transpose(1, 0, 2, 3, 4).reshape(b, s, g, ch), hr, hi


def memory_kv(mem, g_mem, w_mk, w_mv):
    b, n, _ = mem.shape
    m = rms_norm(mem, g_mem)
    k = jnp.einsum('bnd,de->bne', m, w_mk).reshape(b, n, MEM_HEADS, MEM_HEAD_DIM)
    v = jnp.einsum('bnd,de->bne', m, w_mv).reshape(b, n, MEM_HEADS, MEM_HEAD_DIM)
    return k, v


def memory_cross_attention(h, mem_k, mem_v, w_mq, w_mo):
    b, s, _ = h.shape
    q = jnp.einsum('bsd,de->bse', h, w_mq).reshape(b, s, MEM_HEADS, MEM_HEAD_DIM)
    sc = jnp.einsum('bshd,bnhd->bhsn', q, mem_k.astype(q.dtype), preferred_element_type=jnp.float32)
    p = jax.nn.softmax(sc * (1.0 / math.sqrt(MEM_HEAD_DIM)), axis=-1)
    o = jnp.einsum('bhsn,bnhd->bshd', p, mem_v.astype(jnp.float32)).astype(h.dtype)
    return jnp.einsum('bse,ed->bsd', o.reshape(b, s, MEM_HEADS * MEM_HEAD_DIM), w_mo)


def decoder_layer(x, pos, past_k, past_v, past_len, h0_re, h0_im, mem_k, mem_v,
                  q_block, ssm_chunk, ssm_p, w):
    lam_re, lam_im, bbar_re, bbar_im, c_re, c_im, d_skip, w_glu, b_glu = ssm_p
    (g_mix, w_in, g_attn_out, g_ssm_out, w_out, g_cross, w_mq, w_mo,
     g_ffn, w_up, w_down, g_final) = w
    b, s, _ = x.shape
    h = rms_norm(x, g_mix)
    z = jnp.einsum('bsd,de->bse', h, w_in)
    q = rotary(z[..., :D_ATTN].reshape(b, s, N_HEADS, HEAD_DIM), pos)
    k = rotary(z[..., D_ATTN:2 * D_ATTN].reshape(b, s, N_HEADS, HEAD_DIM), pos)
    v = z[..., 2 * D_ATTN:3 * D_ATTN].reshape(b, s, N_HEADS, HEAD_DIM)
    u = z[..., 3 * D_ATTN:]
    k_all = jnp.concatenate([past_k.astype(k.dtype), k], axis=1)
    v_all = jnp.concatenate([past_v.astype(v.dtype), v], axis=1)
    offset = past_len - past_k.shape[1]
    attn = dilated_mixture_attention(q, k_all, v_all, pos, offset, q_block).reshape(b, s, D_ATTN)
    u4 = u.reshape(b, s, N_SSM_GROUPS, SSM_GROUP).astype(jnp.float32)
    y_ssm, hr, hi = s5_scan(u4, lam_re, lam_im, bbar_re, bbar_im, c_re, c_im, h0_re, h0_im, ssm_chunk)
    y_ssm = (y_ssm + d_skip.astype(jnp.float32) * u4).reshape(b, s, D_SSM)
    gl = jax.nn.gelu(y_ssm)
    gate = jax.nn.sigmoid(jnp.einsum('bse,ef->bsf', gl, w_glu.astype(jnp.float32)) + b_glu.astype(jnp.float32))
    y_ssm = (gl * gate).astype(x.dtype)
    mixed = jnp.concatenate([rms_norm(attn, g_attn_out), rms_norm(y_ssm, g_ssm_out)], axis=-1)
    x = x + jnp.einsum('bse,ed->bsd', mixed, w_out)
    x = x + memory_cross_attention(rms_norm(x, g_cross), mem_k, mem_v, w_mq, w_mo)
    hid = jax.nn.relu(jnp.einsum('bsd,df->bsf', rms_norm(x, g_ffn), w_up))
    x = x + jnp.einsum('bsf,fd->bsd', hid * hid, w_down)
    return rms_norm(x, g_final), k, v, hr, hi


def setup_inputs(seed: int = 0) -> dict:
    key = jax.random.key(seed)
    ks = jax.random.split(key, 40)
    f32 = jnp.float32

    def nrm(k, shape, scale):
        return jax.random.normal(k, shape, f32) * scale

    def gain(k, n):
        return 1.0 + 0.01 * jax.random.normal(k, (n,), f32)

    win_len = min(WIN_MAX, PAST_LEN)
    G, P, CH = N_SSM_GROUPS, SSM_STATE, SSM_GROUP
    n_idx = jnp.arange(P, dtype=f32)[None, :]
    return {
        "x_prompt": nrm(ks[0], (BATCH, SEQ, D_MODEL), 1.0),
        "x_sample": nrm(ks[1], (DEC_BATCH, DEC_SEQ, D_MODEL), 1.0),
        "cache_win_k": nrm(ks[2], (DEC_BATCH, win_len, N_HEADS, HEAD_DIM), 1.0),
        "cache_win_v": nrm(ks[3], (DEC_BATCH, win_len, N_HEADS, HEAD_DIM), 1.0),
        "state_ssm_re": nrm(ks[4], (DEC_BATCH, G, P), 0.5),
        "state_ssm_im": nrm(ks[5], (DEC_BATCH, G, P), 0.5),
        "cache_mem_k": nrm(ks[6], (DEC_BATCH, N_MEM, MEM_HEADS, MEM_HEAD_DIM), 1.0),
        "cache_mem_v": nrm(ks[7], (DEC_BATCH, N_MEM, MEM_HEADS, MEM_HEAD_DIM), 1.0),
        "mem_prompt": nrm(ks[8], (BATCH, N_MEM, D_MODEL), 1.0),
        "g_mix": gain(ks[9], D_MODEL),
        "w_in": nrm(ks[10], (D_MODEL, 3 * D_ATTN + D_SSM), D_MODEL ** -0.5),
        "a_re": -0.5 + nrm(ks[11], (G, P), 0.01),
        "a_im": math.pi * n_idx + nrm(ks[12], (G, P), 0.01),
        "log_dt": jax.random.uniform(ks[13], (G,), f32, math.log(DT_MIN), math.log(DT_MAX)),
        "b_re": nrm(ks[14], (G, P, CH), (2 * CH) ** -0.5),
        "b_im": nrm(ks[15], (G, P, CH), (2 * CH) ** -0.5),
        "c_re": nrm(ks[16], (G, CH, P), (2 * P) ** -0.5),
        "c_im": nrm(ks[17], (G, CH, P), (2 * P) ** -0.5),
        "d_skip": nrm(ks[18], (G, CH), 1.0),
        "w_glu": nrm(ks[19], (D_SSM, D_SSM), D_SSM ** -0.5),
        "b_glu": nrm(ks[20], (D_SSM,), 0.01),
        "g_attn_out": gain(ks[21], D_ATTN),
        "g_ssm_out": gain(ks[22], D_SSM),
        "w_out": nrm(ks[23], (D_MODEL, D_MODEL), D_MODEL ** -0.5),
        "g_cross": gain(ks[24], D_MODEL),
        "g_mem": gain(ks[25], D_MODEL),
        "w_mq": nrm(ks[26], (D_MODEL, MEM_HEADS * MEM_HEAD_DIM), D_MODEL ** -0.5),
        "w_mk": nrm(ks[27], (D_MODEL, MEM_HEADS * MEM_HEAD_DIM), D_MODEL ** -0.5),
        "w_mv": nrm(ks[28], (D_MODEL, MEM_HEADS * MEM_HEAD_DIM), D_MODEL ** -0.5),
        "w_mo": nrm(ks[29], (MEM_HEADS * MEM_HEAD_DIM, D_MODEL), (MEM_HEADS * MEM_HEAD_DIM) ** -0.5),
        "g_ffn": gain(ks[30], D_MODEL),
        "w_up": nrm(ks[31], (D_MODEL, D_FF), D_MODEL ** -0.5),
        "w_down": nrm(ks[32], (D_FF, D_MODEL), D_FF ** -0.5),
        "g_final": gain(ks[33], D_MODEL),
    }


def reference(x_prompt, x_sample, cache_win_k, cache_win_v, state_ssm_re, state_ssm_im,
              cache_mem_k, cache_mem_v, mem_prompt,
              g_mix, w_in, a_re, a_im, log_dt, b_re, b_im, c_re, c_im, d_skip, w_glu, b_glu,
              g_attn_out, g_ssm_out, w_out, g_cross, g_mem, w_mq, w_mk, w_mv, w_mo,
              g_ffn, w_up, w_down, g_final):
    lam_re, lam_im, bbar_re, bbar_im = s5_discretise(a_re, a_im, log_dt, b_re, b_im)
    ssm_p = (lam_re, lam_im, bbar_re, bbar_im, c_re, c_im, d_skip, w_glu, b_glu)
    w = (g_mix, w_in, g_attn_out, g_ssm_out, w_out, g_cross, w_mq, w_mo,
         g_ffn, w_up, w_down, g_final)
    for _layer in range(DEPTH):
        bp, sp, _ = x_prompt.shape
        empty = jnp.zeros((bp, 0, N_HEADS, HEAD_DIM), x_prompt.dtype)
        h0 = jnp.zeros((bp, N_SSM_GROUPS, SSM_STATE), jnp.float32)
        mem_k_p, mem_v_p = memory_kv(mem_prompt, g_mem, w_mk, w_mv)
        pos_p = jnp.arange(sp, dtype=jnp.int32)
        y_prompt, k_p, v_p, hr_p, hi_p = decoder_layer(
            x_prompt, pos_p, empty, empty, 0, h0, h0, mem_k_p, mem_v_p,
            min(PROMPT_Q_BLOCK, sp), min(SSM_CHUNK, sp), ssm_p, w)
        keep = min(WIN_MAX, sp)
        qs = x_sample.shape[1]
        pos_s = PAST_LEN + jnp.arange(qs, dtype=jnp.int32)
        y_sample, k_s, v_s, hr_s, hi_s = decoder_layer(
            x_sample, pos_s, cache_win_k, cache_win_v, PAST_LEN, state_ssm_re, state_ssm_im,
            cache_mem_k, cache_mem_v, 1, qs, ssm_p, w)
    return (y_prompt, y_sample, k_p[:, sp - keep:], v_p[:, sp - keep:], hr_p, hi_p, mem_k_p, mem_v_p,
            k_s, v_s, hr_s, hi_s)
```

```python
import functools
import math

import numpy as np
import jax
import jax.numpy as jnp
from jax import lax
from jax.experimental import pallas as pl
from jax.experimental.pallas import tpu as pltpu

f32 = jnp.float32
bf16 = jnp.bfloat16

HEAD_DIM = 128
SSM_GROUP = 16
SSM_STATE = 64
DILATED_BRANCHES = ((128, 1), (512, 4), (2048, 16))
MAX_DIL = 16
PAST_LEN = 2048
MEM_HEADS = 4
MEM_HEAD_DIM = 128
ROPE_THETA = 10000.0
NORM_EPS = 1e-6
NEG_BIG = -1e30

LANES = 128
SUBLANES = 8
SSM_T = 8
GROUPS_PER_BLOCK = LANES // SSM_GROUP
STATE_BLOCK = GROUPS_PER_BLOCK * SSM_STATE
V7X_VMEM_BYTES = 64 * 1024 * 1024


def _cparams(sem, vmem_mb):
    assert vmem_mb * 1024 * 1024 < V7X_VMEM_BYTES
    return pltpu.CompilerParams(dimension_semantics=sem, vmem_limit_bytes=vmem_mb * 1024 * 1024)


def _rmsnorm_kernel(x_ref, g_ref, o_ref):
    x = x_ref[...].astype(f32)
    y = x * lax.rsqrt(jnp.mean(x * x, axis=-1, keepdims=True) + NORM_EPS)
    o_ref[...] = (y * g_ref[...]).astype(o_ref.dtype)


def rmsnorm(x, g, out_dtype, row_start=0, rows=None, tm=256):
    m, d = x.shape
    rows = m if rows is None else rows
    tm = min(tm, rows)
    assert rows % tm == 0 and row_start % tm == 0
    off = row_start // tm
    return pl.pallas_call(
        _rmsnorm_kernel,
        out_shape=jax.ShapeDtypeStruct((rows, d), out_dtype),
        grid=(rows // tm,),
        in_specs=[pl.BlockSpec((tm, d), lambda i: (i + off, 0)),
                  pl.BlockSpec((1, d), lambda i: (0, 0))],
        out_specs=pl.BlockSpec((tm, d), lambda i: (i, 0)),
        compiler_params=_cparams(("parallel",), 40),
        name="rmsnorm",
    )(x, g.reshape(1, d).astype(f32))


def _ep_none(acc):
    return acc


def _ep_residual(acc, res):
    return acc + res


def _ep_relu2(acc):
    r = jnp.maximum(acc, 0.0)
    return r * r


def _ep_glu(acc, gl, bias):
    return gl * jax.nn.sigmoid(acc + bias)


def _ep_rotary(acc, cos2, sin2):
    parts = []
    for h in range(acc.shape[1] // HEAD_DIM):
        blk = acc[:, h * HEAD_DIM:(h + 1) * HEAD_DIM]
        parts.append(blk * cos2 + pltpu.roll(blk, HEAD_DIM // 2, 1) * sin2)
    return jnp.concatenate(parts, axis=-1)


def _mm_kernel(*refs, nk, epilogue, n_extra):
    a_ref, b_ref = refs[0], refs[1]
    extra = refs[2:2 + n_extra]
    o_ref = refs[2 + n_extra]
    part = jnp.dot(a_ref[...].astype(bf16), b_ref[...], preferred_element_type=f32)

    def finish(acc):
        o_ref[...] = epilogue(acc, *[e[...] for e in extra]).astype(o_ref.dtype)

    if nk == 1:
        finish(part)
        return
    acc_ref = refs[3 + n_extra]
    k = pl.program_id(2)

    @pl.when(k == 0)
    def _():
        acc_ref[...] = part

    @pl.when((k > 0) & (k < nk - 1))
    def _():
        acc_ref[...] += part

    @pl.when(k == nk - 1)
    def _():
        finish(acc_ref[...] + part)


def _largest_tile(n, cap):
    best = None
    for t in range(LANES, cap + 1, LANES):
        if n % t == 0:
            best = t
    assert best is not None, (n, cap)
    return best


def matmul(a, b, epilogue=_ep_none, extras=(), out_dtype=f32, name="matmul"):
    m, kdim = a.shape
    _, n = b.shape
    tm = 1024 if m % 1024 == 0 else m
    tn = _largest_tile(n, 1024)
    tk = kdim if kdim <= 4096 else 2048
    assert m % tm == 0 and kdim % tk == 0
    nk = kdim // tk
    in_specs = [pl.BlockSpec((tm, tk), lambda i, j, k: (i, k)),
                pl.BlockSpec((tk, tn), lambda i, j, k: (k, j))]
    ops = [a, b]
    for arr, kind in extras:
        if kind == "tile":
            in_specs.append(pl.BlockSpec((tm, tn), lambda i, j, k: (i, j)))
        elif kind == "col":
            in_specs.append(pl.BlockSpec((1, tn), lambda i, j, k: (0, j)))
        else:
            in_specs.append(pl.BlockSpec((tm, arr.shape[1]), lambda i, j, k: (i, 0)))
        ops.append(arr)
    scratch = [pltpu.VMEM((tm, tn), f32)] if nk > 1 else []
    return pl.pallas_call(
        functools.partial(_mm_kernel, nk=nk, epilogue=epilogue, n_extra=len(extras)),
        out_shape=jax.ShapeDtypeStruct((m, n), out_dtype),
        grid=(m // tm, n // tn, nk),
        in_specs=in_specs,
        out_specs=pl.BlockSpec((tm, tn), lambda i, j, k: (i, j)),
        scratch_shapes=scratch,
        compiler_params=_cparams(("parallel", "parallel", "arbitrary"), 60),
        name=name,
    )(*ops)


def _band_attn_kernel(q_ref, kp_ref, kc_ref, vp_ref, vc_ref, o_ref, *, n_heads, scale):
    a = pl.program_id(1)
    qb = q_ref.shape[0]
    rows = lax.broadcasted_iota(jnp.int32, (qb, 2 * qb), 0)
    cols = lax.broadcasted_iota(jnp.int32, (qb, 2 * qb), 1)
    diff = qb + rows - cols
    valid = (diff >= 0) & (diff <= qb) & ((cols >= qb) | (a > 0))
    lane = lax.broadcasted_iota(jnp.int32, (qb, LANES), 1)
    lse_tile = jnp.zeros((qb, LANES), f32)
    for h in range(n_heads):
        hs = slice(h * HEAD_DIM, (h + 1) * HEAD_DIM)
        q = q_ref[:, hs].astype(bf16)
        k = jnp.concatenate([kp_ref[:, hs], kc_ref[:, hs]], axis=0).astype(bf16)
        v = jnp.concatenate([vp_ref[:, hs], vc_ref[:, hs]], axis=0).astype(bf16)
        s = lax.dot_general(q, k, (((1,), (1,)), ((), ())), preferred_element_type=f32) * scale
        s = jnp.where(valid, s, NEG_BIG)
        m = jnp.max(s, axis=-1, keepdims=True)
        p = jnp.exp(s - m)
        l = jnp.sum(p, axis=-1, keepdims=True)
        o = jnp.dot(p.astype(bf16), v, preferred_element_type=f32) / l
        o_ref[:, hs] = o
        lse_tile = jnp.where(lane == h, m + jnp.log(l), lse_tile)
    o_ref[:, n_heads * HEAD_DIM:] = lse_tile


def band_attention(q, k, v, seq, dil, n_heads):
    m, d = q.shape
    qb = 128
    assert m % dil == 0 and seq % (dil * qb) == 0 and n_heads <= LANES
    qv, kv, vv = (t.reshape(m // dil, dil * d) for t in (q, k, v))
    de = d + LANES
    cur = pl.BlockSpec((qb, d), lambda r, a: (a, r))
    prev = pl.BlockSpec((qb, d), lambda r, a: (jnp.maximum(a - 1, 0), r))
    out = pl.pallas_call(
        functools.partial(_band_attn_kernel, n_heads=n_heads, scale=1.0 / math.sqrt(HEAD_DIM)),
        out_shape=jax.ShapeDtypeStruct((seq // dil, dil * de), f32),
        grid=(dil, seq // dil // qb),
        in_specs=[cur, prev, cur, prev, cur],
        out_specs=pl.BlockSpec((qb, de), lambda r, a: (a, r)),
        compiler_params=_cparams(("parallel", "arbitrary"), 40),
        name=f"band_attn_d{dil}",
    )(qv, kv, kv, vv, vv)
    return out.reshape(seq, de)


def _merge_kernel(*refs, n_heads):
    e_refs, o_ref = refs[:-1], refs[-1]
    d = n_heads * HEAD_DIM
    lses = [e[:, d:] for e in e_refs]
    mx = functools.reduce(jnp.maximum, lses)
    ws = [jnp.exp(l - mx) for l in lses]
    tot = functools.reduce(lambda x, y: x + y, ws)
    ws = [w / tot for w in ws]
    for h in range(n_heads):
        hs = slice(h * HEAD_DIM, (h + 1) * HEAD_DIM)
        acc = ws[0][:, h:h + 1] * e_refs[0][:, hs]
        for w, e in zip(ws[1:], e_refs[1:]):
            acc = acc + w[:, h:h + 1] * e[:, hs]
        o_ref[:, hs] = acc


def merge_branches(exts, n_heads):
    seq, de = exts[0].shape
    d = n_heads * HEAD_DIM
    tm = 256
    return pl.pallas_call(
        functools.partial(_merge_kernel, n_heads=n_heads),
        out_shape=jax.ShapeDtypeStruct((seq, d), f32),
        grid=(seq // tm,),
        in_specs=[pl.BlockSpec((tm, de), lambda i: (i, 0)) for _ in exts],
        out_specs=pl.BlockSpec((tm, d), lambda i: (i, 0)),
        compiler_params=_cparams(("parallel",), 40),
        name="merge_branches",
    )(*exts)


def _sample_key_positions(win_len, dec):
    half = MAX_DIL // 2
    ma, ra = np.meshgrid(np.arange(win_len // MAX_DIL), np.arange(half), indexing="ij")
    pos_a = (MAX_DIL * ma + ra).reshape(-1)
    mb0 = 3 * (win_len // MAX_DIL) // 4
    mb, rb = np.meshgrid(np.arange(mb0, win_len // MAX_DIL), np.arange(half, MAX_DIL), indexing="ij")
    pos_b = (MAX_DIL * mb + rb).reshape(-1)
    return np.concatenate([pos_a, pos_b, win_len + np.arange(dec)])


def _sample_bias(win_len, dec, n_heads):
    kpos = _sample_key_positions(win_len, dec)[:, None]
    qpos = PAST_LEN + (np.arange(LANES) % dec)[None, :]
    delta = qpos - kpos
    out = []
    for window, dil in DILATED_BRANCHES:
        ok = (delta >= 0) & (delta <= window) & (delta % dil == 0) & (kpos >= 0)
        out.append(np.where(ok, 0.0, NEG_BIG))
    out = np.stack(out).astype(np.float32)
    out[:, :, n_heads * dec:] = NEG_BIG
    return out


def _sample_attn_kernel(q_ref, kn_ref, vn_ref, ka_ref, kb_ref, va_ref, vb_ref, bias_ref, o_ref, *, n_heads, scale):
    d = n_heads * HEAD_DIM
    dec = q_ref.shape[0]
    nq = LANES
    q = q_ref[...]
    qt = jnp.concatenate([q] * (nq // dec), axis=0)
    rowh = lax.broadcasted_iota(jnp.int32, (nq, d), 0) // dec
    colh = lax.broadcasted_iota(jnp.int32, (nq, d), 1) // HEAD_DIM
    qbd = jnp.where(rowh == colh, qt, 0.0).astype(bf16)
    na = ka_ref.shape[0] * ka_ref.shape[1]
    nb = kb_ref.shape[0] * kb_ref.shape[1]
    nt = (((1,), (1,)), ((), ()))
    sa = lax.dot_general(ka_ref[...].reshape(na, d).astype(bf16), qbd, nt, preferred_element_type=f32)
    sb = lax.dot_general(kb_ref[...].reshape(nb, d).astype(bf16), qbd, nt, preferred_element_type=f32)
    sn = lax.dot_general(kn_ref[...].astype(bf16), qbd, nt, preferred_element_type=f32)
    s = jnp.concatenate([sa, sb, sn], axis=0) * scale
    ps, ls, lses = [], [], []
    for g in range(len(DILATED_BRANCHES)):
        sg = s + bias_ref[g]
        m = jnp.max(sg, axis=0, keepdims=True)
        p = jnp.exp(sg - m)
        l = jnp.sum(p, axis=0, keepdims=True)
        ps.append(p)
        ls.append(l)
        lses.append(m + jnp.log(l))
    mx = functools.reduce(jnp.maximum, lses)
    ws = [jnp.exp(x - mx) for x in lses]
    tot = functools.reduce(lambda x, y: x + y, ws)
    pm = ps[0] * (ws[0] / (tot * ls[0]))
    for p, w, l in zip(ps[1:], ws[1:], ls[1:]):
        pm = pm + p * (w / (tot * l))
    pm = pm.astype(bf16)
    tn = (((0,), (0,)), ((), ()))
    o = lax.dot_general(pm[:na], va_ref[...].reshape(na, d).astype(bf16), tn, preferred_element_type=f32)
    o = o + lax.dot_general(pm[na:na + nb], vb_ref[...].reshape(nb, d).astype(bf16), tn, preferred_element_type=f32)
    o = o + lax.dot_general(pm[na + nb:], vn_ref[...].astype(bf16), tn, preferred_element_type=f32)
    for h in range(n_heads):
        hs = slice(h * HEAD_DIM, (h + 1) * HEAD_DIM)
        o_ref[:, hs] = o[h * dec:(h + 1) * dec, hs]


def sample_attention(q, k, v, cache_k, cache_v, row0, n_heads):
    m, d = q.shape
    nbat, win_len = cache_k.shape[0], cache_k.shape[1]
    dec = (m - row0) // nbat
    assert win_len == PAST_LEN and dec == SUBLANES and row0 % dec == 0 and n_heads * dec <= LANES
    assert win_len % (4 * MAX_DIL) == 0 and DILATED_BRANCHES[1][0] <= win_len // 4
    half = MAX_DIL // 2
    mgrp = win_len // MAX_DIL
    q3, k3, v3 = (t.reshape(m // dec, dec, d) for t in (q, k, v))
    ck = cache_k.reshape(nbat, mgrp, 2, half, d)
    cv = cache_v.reshape(nbat, mgrp, 2, half, d)
    bias = jnp.asarray(_sample_bias(win_len, dec, n_heads))
    new = pl.BlockSpec((None, dec, d), lambda b: (row0 // dec + b, 0, 0))
    part_a = pl.BlockSpec((None, mgrp, None, half, d), lambda b: (b, 0, 0, 0, 0))
    part_b = pl.BlockSpec((None, mgrp // 4, None, half, d), lambda b: (b, 3, 1, 0, 0))
    out = pl.pallas_call(
        functools.partial(_sample_attn_kernel, n_heads=n_heads, scale=1.0 / math.sqrt(HEAD_DIM)),
        out_shape=jax.ShapeDtypeStruct((nbat, dec, d), f32),
        grid=(nbat,),
        in_specs=[new, new, new, part_a, part_b, part_a, part_b,
                  pl.BlockSpec(bias.shape, lambda b: (0, 0, 0))],
        out_specs=pl.BlockSpec((None, dec, d), lambda b: (b, 0, 0)),
        compiler_params=_cparams(("parallel",), 60),
        name="sample_attn",
    )(q3, k3, v3, ck, ck, cv, cv, bias)
    return out.reshape(nbat * dec, d)


def _s5_params_kernel(ar_r, ai_r, ar_c, ai_c, ldt, br, bi, cr, ci,
                      kt_ref, pr_ref, pi_ref, qr_ref, qi_ref, lr_ref, li_ref):
    dt = jnp.exp(ldt[...])

    def disc(a_re, a_im):
        mag = jnp.exp(a_re * dt)
        lam_re = mag * jnp.cos(a_im * dt)
        lam_im = mag * jnp.sin(a_im * dt)
        num_re = lam_re - 1.0
        num_im = lam_im
        den = a_re * a_re + a_im * a_im
        f_re = (num_re * a_re + num_im * a_im) / den
        f_im = (num_im * a_re - num_re * a_im) / den
        return lam_re, lam_im, f_re, f_im

    lrc, lic, frc, fic = disc(ar_c[...], ai_c[...])
    lrr, lir, _, _ = disc(ar_r[...], ai_r[...])
    b_re, b_im = br[...], bi[...]
    er = frc * b_re - fic * b_im
    ei = frc * b_im + fic * b_re
    c_re, c_im = cr[...], ci[...]
    hi = lax.Precision.HIGHEST
    es = []
    for n in range(SSM_T):
        es.append((er, ei))
        kt_ref[n] = (jnp.dot(c_re, er, precision=hi, preferred_element_type=f32)
                     - jnp.dot(c_im, ei, precision=hi, preferred_element_type=f32))
        er, ei = lrc * er - lic * ei, lrc * ei + lic * er
    for t in range(SSM_T):
        pr_ref[t] = es[SSM_T - 1 - t][0]
        pi_ref[t] = es[SSM_T - 1 - t][1]
    fr, fi = c_re, c_im
    pw_r, pw_i = jnp.ones_like(lrr), jnp.zeros_like(lir)
    for t in range(SSM_T):
        fr, fi = fr * lrr - fi * lir, fr * lir + fi * lrr
        pw_r, pw_i = pw_r * lrr - pw_i * lir, pw_r * lir + pw_i * lrr
        qr_ref[t] = fr
        qi_ref[t] = fi
    lr_ref[...] = pw_r
    li_ref[...] = pw_i


def s5_chunk_operators(a_re, a_im, log_dt, b_re, b_im, c_re, c_im):
    g, p = a_re.shape
    ch = b_re.shape[2]
    t = SSM_T
    args = [a.astype(f32) for a in (a_re, a_im, b_re, b_im, c_re, c_im)]
    a_re, a_im, b_re, b_im, c_re, c_im = args
    row = pl.BlockSpec((None, 1, p), lambda i: (i, 0, 0))
    col = pl.BlockSpec((None, p, 1), lambda i: (i, 0, 0))
    shp = jax.ShapeDtypeStruct
    kt, pr, pi, qr, qi, lr, li = pl.pallas_call(
        _s5_params_kernel,
        out_shape=[shp((g, t, ch, ch), f32), shp((g, t, p, ch), f32), shp((g, t, p, ch), f32),
                   shp((g, t, ch, p), f32), shp((g, t, ch, p), f32), shp((g, 1, p), f32), shp((g, 1, p), f32)],
        grid=(g,),
        in_specs=[row, row, col, col, pl.BlockSpec((None, 1, 1), lambda i: (i, 0, 0)),
                  pl.BlockSpec((None, p, ch), lambda i: (i, 0, 0)), pl.BlockSpec((None, p, ch), lambda i: (i, 0, 0)),
                  pl.BlockSpec((None, ch, p), lambda i: (i, 0, 0)), pl.BlockSpec((None, ch, p), lambda i: (i, 0, 0))],
        out_specs=[pl.BlockSpec((None, t, ch, ch), lambda i: (i, 0, 0, 0)),
                   pl.BlockSpec((None, t, p, ch), lambda i: (i, 0, 0, 0)),
                   pl.BlockSpec((None, t, p, ch), lambda i: (i, 0, 0, 0)),
                   pl.BlockSpec((None, t, ch, p), lambda i: (i, 0, 0, 0)),
                   pl.BlockSpec((None, t, ch, p), lambda i: (i, 0, 0, 0)),
                   row, row],
        compiler_params=_cparams(("parallel",), 32),
        name="s5_params",
    )(a_re.reshape(g, 1, p), a_im.reshape(g, 1, p), a_re.reshape(g, p, 1), a_im.reshape(g, p, 1),
      log_dt.astype(f32).reshape(g, 1, 1), b_re, b_im, c_re, c_im)
    j = GROUPS_PER_BLOCK
    nb = g // j
    eye = jnp.eye(j, dtype=f32)
    lag = np.arange(t)[None, :] - np.arange(t)[:, None]
    ktp = jnp.concatenate([kt, jnp.zeros((g, 1, ch, ch), f32)], axis=1)
    ktt = ktp[:, np.where(lag >= 0, lag, t)]
    ktt = ktt.reshape(nb, j, t, t, ch, ch).transpose(0, 2, 1, 5, 3, 4)
    w = ktt[:, :, :, :, :, None, :] * eye[None, None, :, None, None, :, None]
    w = w.reshape(nb, t * j * ch, t * j * ch).astype(bf16)
    pc = jnp.stack([pr, pi], axis=2)
    pc = pc.reshape(nb, j, t, 2, p, ch).transpose(0, 2, 1, 5, 3, 4)
    pm = pc[:, :, :, :, :, None, :] * eye[None, None, :, None, None, :, None]
    pm = pm.reshape(nb, t * j * ch, 2 * j * p).astype(bf16)
    qc = jnp.stack([qr, -qi], axis=2)
    qc = qc.reshape(nb, j, t, 2, ch, p).transpose(0, 3, 1, 5, 2, 4)
    qm = qc[:, :, :, :, :, None, :] * eye[None, None, :, None, None, :, None]
    qm = qm.reshape(nb, 2 * j * p, t * j * ch).astype(bf16)
    return w, pm, qm, lr.reshape(nb, 1, j * p), li.reshape(nb, 1, j * p)


def _s5_kernel(u_ref, w_ref, p_ref, q_ref, lr_ref, li_ref, dsk_ref, h0r_ref, h0i_ref,
               gl_ref, fpr_ref, fpi_ref, fsr_ref, fsi_ref, h_scr, *, n_chunks, row_chunk):
    rows = u_ref.shape[0]
    sb = STATE_BLOCK
    lr, li = lr_ref[...], li_ref[...]

    def ucat(r0, r1):
        return jnp.concatenate([u_ref[r0:r1, t, :] for t in range(SSM_T)], axis=-1)

    for r0 in range(0, rows, row_chunk):
        r1 = min(r0 + row_chunk, rows)
        h_scr[r0:r1, :] = jnp.dot(ucat(r0, r1).astype(bf16), p_ref[...], preferred_element_type=f32)

    def step(k, carry):
        hr, hi = carry
        inr = h_scr[pl.ds(k, 1), :sb]
        ini = h_scr[pl.ds(k, 1), sb:]
        h_scr[pl.ds(k, 1), :sb] = hr
        h_scr[pl.ds(k, 1), sb:] = hi
        return lr * hr - li * hi + inr, lr * hi + li * hr + ini

    zero = jnp.zeros((1, sb), f32)
    hr, hi = lax.fori_loop(0, n_chunks, step, (zero, zero))
    fpr_ref[...] = hr
    fpi_ref[...] = hi
    h0r, h0i = h0r_ref[...], h0i_ref[...]
    fsr_ref[...] = lr * h0r - li * h0i + h_scr[n_chunks:, :sb]
    fsi_ref[...] = lr * h0i + li * h0r + h_scr[n_chunks:, sb:]
    h_scr[n_chunks:, :sb] = h0r
    h_scr[n_chunks:, sb:] = h0i

    dsk = dsk_ref[...]
    for r0 in range(0, rows, row_chunk):
        r1 = min(r0 + row_chunk, rows)
        u = ucat(r0, r1)
        y = jnp.dot(u.astype(bf16), w_ref[...], preferred_element_type=f32)
        y = y + jnp.dot(h_scr[r0:r1, :].astype(bf16), q_ref[...], preferred_element_type=f32)
        gl = jax.nn.gelu(y + dsk * u)
        for t in range(SSM_T):
            gl_ref[r0:r1, t, :] = gl[:, t * LANES:(t + 1) * LANES]


def s5_mixer(u, ops, d_skip, h0_re, h0_im, n_prompt):
    w, pm, qm, lr, li = ops
    m, dssm = u.shape
    nb = dssm // LANES
    t = SSM_T
    rows = m // t
    n_chunks = n_prompt // t
    nbat = h0_re.shape[0]
    assert rows - n_chunks == nbat and n_prompt % t == 0
    sb = STATE_BLOCK
    u3 = u.reshape(rows, t, dssm)
    dsk = jnp.tile(d_skip.astype(f32).reshape(nb, 1, LANES), (1, 1, t))
    h0r = h0_re.astype(f32).reshape(nbat, nb * sb)
    h0i = h0_im.astype(f32).reshape(nbat, nb * sb)
    blk = lambda shape: pl.BlockSpec((None,) + shape, lambda i: (i, 0, 0))
    ublk = pl.BlockSpec((rows, t, LANES), lambda i: (0, 0, i))
    hblk = pl.BlockSpec((nbat, sb), lambda i: (0, i))
    fblk = pl.BlockSpec((1, sb), lambda i: (0, i))
    shp = jax.ShapeDtypeStruct
    gl, fpr, fpi, fsr, fsi = pl.pallas_call(
        functools.partial(_s5_kernel, n_chunks=n_chunks, row_chunk=384),
        out_shape=[shp((rows, t, dssm), f32), shp((1, nb * sb), f32), shp((1, nb * sb), f32),
                   shp((nbat, nb * sb), f32), shp((nbat, nb * sb), f32)],
        grid=(nb,),
        in_specs=[ublk, blk((t * LANES, t * LANES)), blk((t * LANES, 2 * sb)), blk((2 * sb, t * LANES)),
                  blk((1, sb)), blk((1, sb)), blk((1, t * LANES)), hblk, hblk],
        out_specs=[ublk, fblk, fblk, hblk, hblk],
        scratch_shapes=[pltpu.VMEM((rows, 2 * sb), f32)],
        compiler_params=_cparams(("parallel",), 56),
        name="s5_mixer",
    )(u3, w, pm, qm, lr, li, dsk, h0r, h0i)
    return gl.reshape(m, dssm), fpr, fpi, fsr, fsi


def _cross_attn_kernel(q_ref, k_ref, v_ref, o_ref, *, scale):
    q = q_ref[...].astype(bf16)
    k = k_ref[...].astype(bf16)
    v = v_ref[...].astype(bf16)
    for h in range(MEM_HEADS):
        hs = slice(h * MEM_HEAD_DIM, (h + 1) * MEM_HEAD_DIM)
        s = jnp.einsum("gqd,gkd->gqk", q[:, :, hs], k[:, :, hs], preferred_element_type=f32) * scale
        m = jnp.max(s, axis=-1, keepdims=True)
        p = jnp.exp(s - m)
        p = p / jnp.sum(p, axis=-1, keepdims=True)
        o = jnp.einsum("gqk,gkd->gqd", p.astype(bf16), v[:, :, hs], preferred_element_type=f32)
        o_ref[:, :, hs] = o.astype(o_ref.dtype)


def cross_attention(q3, k3, v3, nq, gb, tq, q_block0, name):
    g, n, e = k3.shape
    return pl.pallas_call(
        functools.partial(_cross_attn_kernel, scale=1.0 / math.sqrt(MEM_HEAD_DIM)),
        out_shape=jax.ShapeDtypeStruct((g, nq, e), bf16),
        grid=(g // gb, nq // tq),
        in_specs=[pl.BlockSpec((gb, tq, e), lambda i, j: (q_block0 + i, j, 0)),
                  pl.BlockSpec((gb, n, e), lambda i, j: (i, 0, 0)),
                  pl.BlockSpec((gb, n, e), lambda i, j: (i, 0, 0))],
        out_specs=pl.BlockSpec((gb, tq, e), lambda i, j: (i, j, 0)),
        compiler_params=_cparams(("parallel", "parallel"), 48),
        name=name,
    )(q3, k3, v3)


def _rotary_tables(pos):
    inv = 1.0 / (ROPE_THETA ** (jnp.arange(0, HEAD_DIM, 2, dtype=f32) / HEAD_DIM))
    ang = pos.astype(f32)[:, None] * inv[None, :]
    cos, sin = jnp.cos(ang), jnp.sin(ang)
    return jnp.concatenate([cos, cos], axis=-1), jnp.concatenate([-sin, sin], axis=-1)


def kernel(x_prompt, x_sample, cache_win_k, cache_win_v, state_ssm_re, state_ssm_im, cache_mem_k, cache_mem_v,
           mem_prompt, g_mix, w_in, a_re, a_im, log_dt, b_re, b_im, c_re, c_im, d_skip, w_glu, b_glu,
           g_attn_out, g_ssm_out, w_out, g_cross, g_mem, w_mq, w_mk, w_mv, w_mo, g_ffn, w_up, w_down, g_final):
    bp, sp, dm = x_prompt.shape
    nbat, dec, _ = x_sample.shape
    assert bp == 1
    ns = nbat * dec
    n_groups = a_re.shape[0]
    d_ssm = n_groups * SSM_GROUP
    d_attn = dm - d_ssm
    n_heads = d_attn // HEAD_DIM
    e_mem = MEM_HEADS * MEM_HEAD_DIM
    n_mem = mem_prompt.shape[1]
    keep = min(PAST_LEN, sp)

    x = jnp.concatenate([x_prompt.reshape(sp, dm), x_sample.reshape(ns, dm)], axis=0)
    mtot = sp + ns
    pos = jnp.concatenate([jnp.arange(sp, dtype=jnp.int32), PAST_LEN + jnp.tile(jnp.arange(dec, dtype=jnp.int32), nbat)])
    cos2, sin2 = _rotary_tables(pos)
    wb = lambda w_: w_.astype(bf16)

    h = rmsnorm(x, g_mix, bf16)
    rot = ((cos2, "rowtab"), (sin2, "rowtab"))
    q = matmul(h, wb(w_in[:, :d_attn]), _ep_rotary, rot, name="proj_q")
    k = matmul(h, wb(w_in[:, d_attn:2 * d_attn]), _ep_rotary, rot, name="proj_k")
    v = matmul(h, wb(w_in[:, 2 * d_attn:3 * d_attn]), name="proj_v")
    u = matmul(h, wb(w_in[:, 3 * d_attn:]), name="proj_u")

    exts = [band_attention(q, k, v, sp, dil, n_heads) for _, dil in DILATED_BRANCHES]
    attn_p = merge_branches(exts, n_heads)
    attn_s = sample_attention(q, k, v, cache_win_k, cache_win_v, sp, n_heads)
    na = jnp.concatenate([rmsnorm(attn_p, g_attn_out, bf16), rmsnorm(attn_s, g_attn_out, bf16)], axis=0)

    ops = s5_chunk_operators(a_re, a_im, log_dt, b_re, b_im, c_re, c_im)
    gl, fpr, fpi, fsr, fsi = s5_mixer(u, ops, d_skip, state_ssm_re, state_ssm_im, sp)
    y_ssm = matmul(gl, wb(w_glu), _ep_glu, ((gl, "tile"), (b_glu.astype(f32).reshape(1, d_ssm), "col")), name="glu")
    ny = rmsnorm(y_ssm, g_ssm_out, bf16)

    mixed = jnp.concatenate([na, ny], axis=-1)
    x1 = matmul(mixed, wb(w_out), _ep_residual, ((x, "tile"),), name="out_proj")

    mem_n = rmsnorm(mem_prompt.reshape(n_mem, dm), g_mem, bf16)
    mem_k_p = matmul(mem_n, wb(w_mk), name="mem_k")
    mem_v_p = matmul(mem_n, wb(w_mv), name="mem_v")
    hc = rmsnorm(x1, g_cross, bf16)
    qm = matmul(hc, wb(w_mq), name="mem_q")
    o_p = cross_attention(qm.reshape(1, mtot, e_mem), mem_k_p.reshape(1, n_mem, e_mem), mem_v_p.reshape(1, n_mem, e_mem),
                          sp, 1, 512, 0, "cross_attn_prompt")
    o_s = cross_attention(qm.reshape(mtot // dec, dec, e_mem), cache_mem_k.reshape(nbat, n_mem, e_mem),
                          cache_mem_v.reshape(nbat, n_mem, e_mem), dec, 8, dec, sp // dec // 8, "cross_attn_sample")
    o_c = jnp.concatenate([o_p.reshape(sp, e_mem), o_s.reshape(ns, e_mem)], axis=0)
    x2 = matmul(o_c, wb(w_mo), _ep_residual, ((x1, "tile"),), name="mem_out")

    hf = rmsnorm(x2, g_ffn, bf16)
    hid = matmul(hf, wb(w_up), _ep_relu2, out_dtype=bf16, name="ffn_up")
    x3 = matmul(hid, wb(w_down), _ep_residual, ((x2, "tile"),), name="ffn_down")
    y_p = rmsnorm(x3, g_final, f32, 0, sp)
    y_s = rmsnorm(x3, g_final, f32, sp, ns)

    hd = (n_heads, HEAD_DIM)
    return (y_p.reshape(1, sp, dm), y_s.reshape(nbat, dec, dm),
            k[sp - keep:sp].reshape(1, keep, *hd), v[sp - keep:sp].reshape(1, keep, *hd),
            fpr.reshape(1, n_groups, SSM_STATE), fpi.reshape(1, n_groups, SSM_STATE),
            mem_k_p.reshape(1, n_mem, MEM_HEADS, MEM_HEAD_DIM), mem_v_p.reshape(1, n_mem, MEM_HEADS, MEM_HEAD_DIM),
            k[sp:].reshape(nbat, dec, *hd), v[sp:].reshape(nbat, dec, *hd),
            fsr.reshape(nbat, n_groups, SSM_STATE), fsi.reshape(nbat, n_groups, SSM_STATE))
```

```python
import functools
import math

import numpy as np
import jax
import jax.numpy as jnp
from jax import lax
from jax.experimental import pallas as pl
from jax.experimental.pallas import tpu as pltpu

f32 = jnp.float32
bf16 = jnp.bfloat16

HEAD_DIM = 128
SSM_GROUP = 16
SSM_STATE = 64
DILATED_BRANCHES = ((128, 1), (512, 4), (2048, 16))
MAX_DIL = 16
PAST_LEN = 2048
MEM_HEADS = 4
MEM_HEAD_DIM = 128
ROPE_THETA = 10000.0
NORM_EPS = 1e-6
NEG_BIG = -1e30

LANES = 128
SUBLANES = 8
SSM_T = 8
GROUPS_PER_BLOCK = LANES // SSM_GROUP
STATE_BLOCK = GROUPS_PER_BLOCK * SSM_STATE
ATTN_QB = 128
V7X_VMEM_BYTES = 64 * 1024 * 1024


def _cparams(sem, vmem_mb):
    assert vmem_mb * 1024 * 1024 < V7X_VMEM_BYTES
    return pltpu.CompilerParams(dimension_semantics=sem, vmem_limit_bytes=vmem_mb * 1024 * 1024)


def _rmsnorm_kernel(x_ref, g_ref, o_ref):
    x = x_ref[...].astype(f32)
    y = x * lax.rsqrt(jnp.mean(x * x, axis=-1, keepdims=True) + NORM_EPS)
    o_ref[...] = (y * g_ref[...]).astype(o_ref.dtype)


def rmsnorm(x, g, out_dtype, row_start=0, rows=None, tm=256):
    m, d = x.shape
    rows = m if rows is None else rows
    tm = min(tm, rows)
    assert rows % tm == 0 and row_start % tm == 0
    off = row_start // tm
    return pl.pallas_call(
        _rmsnorm_kernel,
        out_shape=jax.ShapeDtypeStruct((rows, d), out_dtype),
        grid=(rows // tm,),
        in_specs=[pl.BlockSpec((tm, d), lambda i: (i + off, 0)),
                  pl.BlockSpec((1, d), lambda i: (0, 0))],
        out_specs=pl.BlockSpec((tm, d), lambda i: (i, 0)),
        compiler_params=_cparams(("parallel",), 40),
        name="rmsnorm",
    )(x, g.reshape(1, d).astype(f32))


def _rmsnorm2_kernel(xa_ref, xb_ref, g_ref, o_ref, *, na):
    i = pl.program_id(0)

    @pl.when(i < na)
    def _():
        _rmsnorm_kernel(xa_ref, g_ref, o_ref)

    @pl.when(i >= na)
    def _():
        _rmsnorm_kernel(xb_ref, g_ref, o_ref)


def rmsnorm2(xa, xb, g, out_dtype, tm=256):
    (ma, d), (mb, _) = xa.shape, xb.shape
    assert ma % tm == 0 and mb % tm == 0
    na = ma // tm
    return pl.pallas_call(
        functools.partial(_rmsnorm2_kernel, na=na),
        out_shape=jax.ShapeDtypeStruct((ma + mb, d), out_dtype),
        grid=((ma + mb) // tm,),
        in_specs=[pl.BlockSpec((tm, d), lambda i: (jnp.minimum(i, na - 1), 0)),
                  pl.BlockSpec((tm, d), lambda i: (jnp.maximum(i - na, 0), 0)),
                  pl.BlockSpec((1, d), lambda i: (0, 0))],
        out_specs=pl.BlockSpec((tm, d), lambda i: (i, 0)),
        compiler_params=_cparams(("arbitrary",), 40),
        name="rmsnorm2",
    )(xa, xb, g.reshape(1, d).astype(f32))


def _ep_none(acc):
    return acc


def _ep_residual(acc, res):
    return acc + res


def _ep_relu2(acc):
    r = jnp.maximum(acc, 0.0)
    return r * r


def _ep_glu(acc, gl, bias):
    return gl * jax.nn.sigmoid(acc + bias)


def _ep_rotary(acc, cos2, sin2):
    parts = []
    for h in range(acc.shape[1] // HEAD_DIM):
        blk = acc[:, h * HEAD_DIM:(h + 1) * HEAD_DIM]
        parts.append(blk * cos2 + pltpu.roll(blk, HEAD_DIM // 2, 1) * sin2)
    return jnp.concatenate(parts, axis=-1)


def _mm_kernel(*refs, nk, epilogue, n_extra, head_major):
    a_ref, b_ref = refs[0], refs[1]
    extra = refs[2:2 + n_extra]
    o_ref = refs[2 + n_extra]
    part = jnp.dot(a_ref[...].astype(bf16), b_ref[...], preferred_element_type=f32)

    def finish(acc):
        res = epilogue(acc, *[e[...] for e in extra]).astype(o_ref.dtype)
        if head_major:
            for h in range(o_ref.shape[0]):
                o_ref[h] = res[:, h * LANES:(h + 1) * LANES]
        else:
            o_ref[...] = res

    if nk == 1:
        finish(part)
        return
    acc_ref = refs[3 + n_extra]
    k = pl.program_id(2)

    @pl.when(k == 0)
    def _():
        acc_ref[...] = part

    @pl.when((k > 0) & (k < nk - 1))
    def _():
        acc_ref[...] += part

    @pl.when(k == nk - 1)
    def _():
        finish(acc_ref[...] + part)


def _largest_tile(n, cap):
    best = None
    for t in range(LANES, cap + 1, LANES):
        if n % t == 0:
            best = t
    assert best is not None, (n, cap)
    return best


def matmul(a, b, epilogue=_ep_none, extras=(), out_dtype=f32, head_major=False, cols=None, name="matmul"):
    m, kdim = a.shape
    col0, n = (0, b.shape[1]) if cols is None else cols
    tm = 1024 if m % 1024 == 0 else m
    tn = _largest_tile(math.gcd(n, col0) if col0 else n, 1024)
    tk = kdim if kdim <= 4096 else 2048
    assert m % tm == 0 and kdim % tk == 0 and col0 % tn == 0 and n % tn == 0
    nk = kdim // tk
    joff = col0 // tn
    in_specs = [pl.BlockSpec((tm, tk), lambda i, j, k: (i, k)),
                pl.BlockSpec((tk, tn), lambda i, j, k: (k, j + joff))]
    ops = [a, b]
    for arr, kind in extras:
        if kind == "tile":
            in_specs.append(pl.BlockSpec((tm, tn), lambda i, j, k: (i, j)))
        elif kind == "col":
            in_specs.append(pl.BlockSpec((1, tn), lambda i, j, k: (0, j)))
        else:
            in_specs.append(pl.BlockSpec((tm, arr.shape[1]), lambda i, j, k: (i, 0)))
        ops.append(arr)
    scratch = [pltpu.VMEM((tm, tn), f32)] if nk > 1 else []
    if head_major:
        out_shape = jax.ShapeDtypeStruct((n // LANES, m, LANES), out_dtype)
        out_spec = pl.BlockSpec((tn // LANES, tm, LANES), lambda i, j, k: (j, i, 0))
    else:
        out_shape = jax.ShapeDtypeStruct((m, n), out_dtype)
        out_spec = pl.BlockSpec((tm, tn), lambda i, j, k: (i, j))
    return pl.pallas_call(
        functools.partial(_mm_kernel, nk=nk, epilogue=epilogue, n_extra=len(extras), head_major=head_major),
        out_shape=out_shape,
        grid=(m // tm, n // tn, nk),
        in_specs=in_specs,
        out_specs=out_spec,
        scratch_shapes=scratch,
        compiler_params=_cparams(("parallel", "parallel", "arbitrary"), 60),
        name=name,
    )(*ops)


def _out_proj_kernel(na_ref, ny_ref, wa_ref, wy_ref, xa_ref, xb_ref, o_ref, *, nblk_a):
    i = pl.program_id(0)
    acc = jnp.dot(na_ref[...], wa_ref[...], preferred_element_type=f32)
    acc = acc + jnp.dot(ny_ref[...], wy_ref[...], preferred_element_type=f32)

    @pl.when(i < nblk_a)
    def _():
        o_ref[...] = acc + xa_ref[...]

    @pl.when(i >= nblk_a)
    def _():
        o_ref[...] = acc + xb_ref[...]


def out_projection(na, ny, w_a, w_y, xa, xb):
    m, ka = na.shape
    ky = ny.shape[1]
    n = w_a.shape[1]
    tm = 1024
    tn = _largest_tile(n, 1024)
    assert m % tm == 0 and xa.shape[0] % tm == 0 and xa.shape[0] + xb.shape[0] == m
    nblk_a = xa.shape[0] // tm
    last_j = n // tn - 1
    xa_spec = pl.BlockSpec((tm, tn), lambda i, j: (jnp.minimum(i, nblk_a - 1), jnp.where(i < nblk_a, j, last_j)))
    xb_spec = pl.BlockSpec((tm, tn), lambda i, j: (jnp.maximum(i - nblk_a, 0), jnp.where(i < nblk_a, 0, j)))
    return pl.pallas_call(
        functools.partial(_out_proj_kernel, nblk_a=nblk_a),
        out_shape=jax.ShapeDtypeStruct((m, n), f32),
        grid=(m // tm, n // tn),
        in_specs=[pl.BlockSpec((tm, ka), lambda i, j: (i, 0)), pl.BlockSpec((tm, ky), lambda i, j: (i, 0)),
                  pl.BlockSpec((ka, tn), lambda i, j: (0, j)), pl.BlockSpec((ky, tn), lambda i, j: (0, j)),
                  xa_spec, xb_spec],
        out_specs=pl.BlockSpec((tm, tn), lambda i, j: (i, j)),
        compiler_params=_cparams(("arbitrary", "arbitrary"), 60),
        name="out_proj",
    )(na, ny, w_a, w_y, xa, xb)


def _prompt_attn_kernel(q_ref, kp_ref, kc_ref, vp_ref, vc_ref, o_ref, kk, vv, ob, lb, *, scale):
    c = pl.program_id(1)
    hg, blk, _ = q_ref.shape
    qb = ATTN_QB
    rows = lax.broadcasted_iota(jnp.int32, (qb, 2 * qb), 0)
    cols = lax.broadcasted_iota(jnp.int32, (qb, 2 * qb), 1)
    diff = qb + rows - cols
    band = (diff >= 0) & (diff <= qb)
    for h in range(hg):
        kk[0:blk, :] = kp_ref[h]
        kk[blk:2 * blk, :] = kc_ref[h]
        vv[0:blk, :] = vp_ref[h]
        vv[blk:2 * blk, :] = vc_ref[h]
        for g, (_, dil) in enumerate(DILATED_BRANCHES):

            def tile(t, carry, g=g, dil=dil):
                r = t % dil
                a = t // dil
                start = r + dil * qb * a
                q = q_ref[h, pl.ds(start, qb, stride=dil), :].astype(bf16)
                kstart = blk + start - dil * qb
                k = kk[pl.ds(kstart, 2 * qb, stride=dil), :].astype(bf16)
                v = vv[pl.ds(kstart, 2 * qb, stride=dil), :].astype(bf16)
                s = lax.dot_general(q, k, (((1,), (1,)), ((), ())), preferred_element_type=f32) * scale
                first_col = jnp.where((a > 0) | (c > 0), 0, qb)
                s = jnp.where(band & (cols >= first_col), s, NEG_BIG)
                m = jnp.max(s, axis=-1, keepdims=True)
                p = jnp.exp(s - m)
                l = jnp.sum(p, axis=-1, keepdims=True)
                o = jnp.dot(p.astype(bf16), v, preferred_element_type=f32) / l
                ob[g, pl.ds(start, qb, stride=dil), :] = o
                lb[g, pl.ds(start, qb, stride=dil), :] = jnp.broadcast_to(m + jnp.log(l), (qb, LANES))
                return carry

            lax.fori_loop(0, blk // qb, tile, 0, unroll=8)
        step = 256
        for r0 in range(0, blk, step):
            ls = [lb[g, r0:r0 + step, :] for g in range(len(DILATED_BRANCHES))]
            mx = functools.reduce(jnp.maximum, ls)
            ws = [jnp.exp(x - mx) for x in ls]
            tot = functools.reduce(lambda x, y: x + y, ws)
            acc = ws[0] * ob[0, r0:r0 + step, :]
            for g in range(1, len(DILATED_BRANCHES)):
                acc = acc + ws[g] * ob[g, r0:r0 + step, :]
            o_ref[r0:r0 + step, h * HEAD_DIM:(h + 1) * HEAD_DIM] = acc / tot


def prompt_attention(q, k, v, seq):
    n_heads, m, _ = q.shape
    blk = MAX_DIL * ATTN_QB
    hg = 2
    assert seq % blk == 0 and n_heads % hg == 0 and m >= seq
    assert all(w == dil * ATTN_QB for w, dil in DILATED_BRANCHES)
    cur = pl.BlockSpec((hg, blk, HEAD_DIM), lambda i, c: (i, c, 0))
    prev = pl.BlockSpec((hg, blk, HEAD_DIM), lambda i, c: (i, jnp.maximum(c - 1, 0), 0))
    nbr = len(DILATED_BRANCHES)
    return pl.pallas_call(
        functools.partial(_prompt_attn_kernel, scale=1.0 / math.sqrt(HEAD_DIM)),
        out_shape=jax.ShapeDtypeStruct((seq, n_heads * HEAD_DIM), f32),
        grid=(n_heads // hg, seq // blk),
        in_specs=[cur, prev, cur, prev, cur],
        out_specs=pl.BlockSpec((blk, hg * HEAD_DIM), lambda i, c: (c, i)),
        scratch_shapes=[pltpu.VMEM((2 * blk, HEAD_DIM), f32), pltpu.VMEM((2 * blk, HEAD_DIM), f32),
                        pltpu.VMEM((nbr, blk, HEAD_DIM), f32), pltpu.VMEM((nbr, blk, LANES), f32)],
        compiler_params=_cparams(("parallel", "arbitrary"), 48),
        name="prompt_attn",
    )(q, k, k, v, v)


def _sample_key_positions(win_len, dec):
    half = MAX_DIL // 2
    ma, ra = np.meshgrid(np.arange(win_len // MAX_DIL), np.arange(half), indexing="ij")
    pos_a = (MAX_DIL * ma + ra).reshape(-1)
    mb0 = 3 * (win_len // MAX_DIL) // 4
    mb, rb = np.meshgrid(np.arange(mb0, win_len // MAX_DIL), np.arange(half, MAX_DIL), indexing="ij")
    pos_b = (MAX_DIL * mb + rb).reshape(-1)
    return np.concatenate([pos_a, pos_b, win_len + np.arange(dec)])


def _sample_bias(win_len, dec, n_heads):
    kpos = _sample_key_positions(win_len, dec)[:, None]
    qpos = PAST_LEN + (np.arange(LANES) % dec)[None, :]
    delta = qpos - kpos
    out = []
    for window, dil in DILATED_BRANCHES:
        ok = (delta >= 0) & (delta <= window) & (delta % dil == 0) & (kpos >= 0)
        out.append(np.where(ok, 0.0, NEG_BIG))
    out = np.stack(out).astype(np.float32)
    out[:, :, n_heads * dec:] = NEG_BIG
    return out


def _sample_attn_kernel(q_ref, kn_ref, vn_ref, ka_ref, kb_ref, va_ref, vb_ref, bias_ref, o_ref, *, scale):
    n_heads, dec, _ = q_ref.shape
    d = n_heads * HEAD_DIM
    nq = LANES

    def gather(ref):
        n = ref.shape[1] * ref.shape[2]
        return jnp.concatenate([ref[h].reshape(n, HEAD_DIM) for h in range(n_heads)], axis=-1).astype(bf16)

    def natural(ref):
        return jnp.concatenate([ref[h] for h in range(n_heads)], axis=-1)

    qt = jnp.concatenate([natural(q_ref)] * (nq // dec), axis=0)
    rowh = lax.broadcasted_iota(jnp.int32, (nq, d), 0) // dec
    colh = lax.broadcasted_iota(jnp.int32, (nq, d), 1) // HEAD_DIM
    qbd = jnp.where(rowh == colh, qt, 0.0).astype(bf16)
    nt = (((1,), (1,)), ((), ()))
    sa = lax.dot_general(gather(ka_ref), qbd, nt, preferred_element_type=f32)
    sb = lax.dot_general(gather(kb_ref), qbd, nt, preferred_element_type=f32)
    sn = lax.dot_general(natural(kn_ref).astype(bf16), qbd, nt, preferred_element_type=f32)
    na, nb = sa.shape[0], sb.shape[0]
    s = jnp.concatenate([sa, sb, sn], axis=0) * scale
    ps, ls, lses = [], [], []
    for g in range(len(DILATED_BRANCHES)):
        sg = s + bias_ref[g]
        m = jnp.max(sg, axis=0, keepdims=True)
        p = jnp.exp(sg - m)
        l = jnp.sum(p, axis=0, keepdims=True)
        ps.append(p)
        ls.append(l)
        lses.append(m + jnp.log(l))
    mx = functools.reduce(jnp.maximum, lses)
    ws = [jnp.exp(x - mx) for x in lses]
    tot = functools.reduce(lambda x, y: x + y, ws)
    pm = ps[0] * (ws[0] / (tot * ls[0]))
    for p, w, l in zip(ps[1:], ws[1:], ls[1:]):
        pm = pm + p * (w / (tot * l))
    pm = pm.astype(bf16)
    tn = (((0,), (0,)), ((), ()))
    o = lax.dot_general(pm[:na], gather(va_ref), tn, preferred_element_type=f32)
    o = o + lax.dot_general(pm[na:na + nb], gather(vb_ref), tn, preferred_element_type=f32)
    o = o + lax.dot_general(pm[na + nb:], natural(vn_ref).astype(bf16), tn, preferred_element_type=f32)
    for h in range(n_heads):
        hs = slice(h * HEAD_DIM, (h + 1) * HEAD_DIM)
        o_ref[:, hs] = o[h * dec:(h + 1) * dec, hs]


def sample_attention(q, k, v, cache_k, cache_v, row0):
    n_heads, m, _ = q.shape
    nbat, win_len = cache_k.shape[0], cache_k.shape[1]
    dec = (m - row0) // nbat
    assert win_len == PAST_LEN and dec == SUBLANES and row0 % dec == 0 and n_heads * dec <= LANES
    assert win_len % (4 * MAX_DIL) == 0 and DILATED_BRANCHES[1][0] <= win_len // 4
    half = MAX_DIL // 2
    mgrp = win_len // MAX_DIL
    d = n_heads * HEAD_DIM
    ck = cache_k.transpose(0, 2, 1, 3).reshape(nbat, n_heads, mgrp, MAX_DIL, HEAD_DIM)
    cv = cache_v.transpose(0, 2, 1, 3).reshape(nbat, n_heads, mgrp, MAX_DIL, HEAD_DIM)
    bias = jnp.asarray(_sample_bias(win_len, dec, n_heads))
    new = pl.BlockSpec((n_heads, dec, HEAD_DIM), lambda b: (0, row0 // dec + b, 0))
    part_a = pl.BlockSpec((None, n_heads, mgrp, half, HEAD_DIM), lambda b: (b, 0, 0, 0, 0))
    part_b = pl.BlockSpec((None, n_heads, mgrp // 4, half, HEAD_DIM), lambda b: (b, 0, 3, 1, 0))
    out = pl.pallas_call(
        functools.partial(_sample_attn_kernel, scale=1.0 / math.sqrt(HEAD_DIM)),
        out_shape=jax.ShapeDtypeStruct((nbat, dec, d), f32),
        grid=(nbat,),
        in_specs=[new, new, new, part_a, part_b, part_a, part_b, pl.BlockSpec(bias.shape, lambda b: (0, 0, 0))],
        out_specs=pl.BlockSpec((None, dec, d), lambda b: (b, 0, 0)),
        compiler_params=_cparams(("parallel",), 56),
        name="sample_attn",
    )(q, k, v, ck, ck, cv, cv, bias)
    return out.reshape(nbat * dec, d)


def _s5_discretise(a_re, a_im, dt):
    mag = jnp.exp(a_re * dt)
    lam_re = mag * jnp.cos(a_im * dt)
    lam_im = mag * jnp.sin(a_im * dt)
    num_re = lam_re - 1.0
    num_im = lam_im
    den = a_re * a_re + a_im * a_im
    f_re = (num_re * a_re + num_im * a_im) / den
    f_im = (num_im * a_re - num_re * a_im) / den
    return lam_re, lam_im, f_re, f_im


def _cmul(ar, ai, br, bi):
    return ar * br - ai * bi, ar * bi + ai * br


def _s5_operators_kernel(arc, aic, ldc, btr, bti, ccr, cci, arb, aib, ldb, cbr, cbi, ars, ais, lds,
                         w_ref, p_ref, q_ref, lr_ref, li_ref):
    t_len = SSM_T
    nt = (((1,), (1,)), ((), ()))
    hi = lax.Precision.HIGHEST
    gsz, ssz = SSM_GROUP, SSM_STATE

    def group_mask(shape, row_div, col_div):
        r = lax.broadcasted_iota(jnp.int32, shape, 0) // row_div
        c = lax.broadcasted_iota(jnp.int32, shape, 1) // col_div
        return r == c

    lr, li, fr, fi = _s5_discretise(arc[...], aic[...], jnp.exp(ldc[...]))
    er, ei = _cmul(fr, fi, btr[...], bti[...])
    c_re, c_im = ccr[...], cci[...]
    mask_w = group_mask((LANES, LANES), gsz, gsz)
    mask_p = group_mask((LANES, STATE_BLOCK), gsz, ssz)
    zero_blk = jnp.zeros((LANES, LANES), bf16)
    d_blocks, e_pows = [], []
    for n in range(t_len):
        e_pows.append((er, ei))
        dn = (lax.dot_general(er, c_re, nt, precision=hi, preferred_element_type=f32)
              - lax.dot_general(ei, c_im, nt, precision=hi, preferred_element_type=f32))
        d_blocks.append(jnp.where(mask_w, dn, 0.0).astype(bf16))
        er, ei = _cmul(lr, li, er, ei)
    for ti in range(t_len):
        for to in range(t_len):
            w_ref[ti * LANES:(ti + 1) * LANES, to * LANES:(to + 1) * LANES] = (
                d_blocks[to - ti] if to >= ti else zero_blk)
        pr, pi = e_pows[t_len - 1 - ti]
        p_ref[ti * LANES:(ti + 1) * LANES, 0:STATE_BLOCK] = jnp.where(
            mask_p, jnp.concatenate([pr] * GROUPS_PER_BLOCK, axis=1), 0.0).astype(bf16)
        p_ref[ti * LANES:(ti + 1) * LANES, STATE_BLOCK:] = jnp.where(
            mask_p, jnp.concatenate([pi] * GROUPS_PER_BLOCK, axis=1), 0.0).astype(bf16)
    lbr, lbi, _, _ = _s5_discretise(arb[...], aib[...], jnp.exp(ldb[...]))
    mask_q = group_mask((STATE_BLOCK, LANES), ssz, gsz)
    qr, qi = cbr[...], cbi[...]
    for t in range(t_len):
        qr, qi = _cmul(qr, qi, lbr, lbi)
        q_ref[0:STATE_BLOCK, t * LANES:(t + 1) * LANES] = jnp.where(
            mask_q, jnp.concatenate([qr] * GROUPS_PER_BLOCK, axis=0), 0.0).astype(bf16)
        q_ref[STATE_BLOCK:, t * LANES:(t + 1) * LANES] = jnp.where(
            mask_q, jnp.concatenate([-qi] * GROUPS_PER_BLOCK, axis=0), 0.0).astype(bf16)
    lsr, lsi, _, _ = _s5_discretise(ars[...], ais[...], jnp.exp(lds[...]))
    pr, pi = lsr, lsi
    for _ in range(t_len - 1):
        pr, pi = _cmul(pr, pi, lsr, lsi)
    lr_ref[...] = pr
    li_ref[...] = pi


def s5_chunk_operators(a_re, a_im, log_dt, b_re, b_im, c_re, c_im):
    g, p = a_re.shape
    ch = b_re.shape[2]
    j = GROUPS_PER_BLOCK
    nb = g // j
    assert p == SSM_STATE and ch == SSM_GROUP and g % j == 0
    t = SSM_T
    a_re, a_im, log_dt, b_re, b_im, c_re, c_im = (x.astype(f32) for x in (a_re, a_im, log_dt, b_re, b_im, c_re, c_im))
    by_row = lambda x: jnp.repeat(x.reshape(nb, j, p), ch, axis=1)
    by_col = lambda x: jnp.repeat(x.reshape(nb, j, p).transpose(0, 2, 1), ch, axis=2)
    ld = log_dt.reshape(nb, j)
    ld_row = jnp.repeat(ld, ch, axis=1)[:, :, None]
    ld_col = jnp.repeat(ld, ch, axis=1)[:, None, :]
    ld_flat = jnp.repeat(ld, p, axis=1)[:, None, :]
    bt = lambda x: x.reshape(nb, j, p, ch).transpose(0, 1, 3, 2).reshape(nb, j * ch, p)
    c_row = lambda x: x.reshape(nb, j * ch, p)
    c_col = lambda x: x.reshape(nb, j, ch, p).transpose(0, 3, 1, 2).reshape(nb, p, j * ch)
    flat = lambda x: x.reshape(nb, 1, j * p)
    ops = [by_row(a_re), by_row(a_im), ld_row, bt(b_re), bt(b_im), c_row(c_re), c_row(c_im),
           by_col(a_re), by_col(a_im), ld_col, c_col(c_re), c_col(c_im),
           flat(a_re), flat(a_im), ld_flat]
    blk = lambda arr: pl.BlockSpec((None,) + arr.shape[1:], lambda i: (i, 0, 0))
    shp = jax.ShapeDtypeStruct
    sb = STATE_BLOCK
    outs = [shp((nb, t * LANES, t * LANES), bf16), shp((nb, t * LANES, 2 * sb), bf16),
            shp((nb, 2 * sb, t * LANES), bf16), shp((nb, 1, sb), f32), shp((nb, 1, sb), f32)]
    return pl.pallas_call(
        _s5_operators_kernel,
        out_shape=outs,
        grid=(nb,),
        in_specs=[blk(x) for x in ops],
        out_specs=[blk(x) for x in outs],
        compiler_params=_cparams(("parallel",), 32),
        name="s5_operators",
    )(*ops)


def _s5_kernel(u_ref, w_ref, p_ref, q_ref, lr_ref, li_ref, dsk_ref, h0r_ref, h0i_ref,
               gl_ref, fpr_ref, fpi_ref, fsr_ref, fsi_ref, h_scr, *, n_chunks, row_chunk):
    rows = u_ref.shape[0] // SSM_T
    sb = STATE_BLOCK
    lr, li = lr_ref[...], li_ref[...]

    def ucat(r0, r1):
        return jnp.concatenate([u_ref[pl.ds(SSM_T * r0 + t, r1 - r0, stride=SSM_T), :] for t in range(SSM_T)], axis=-1)

    for r0 in range(0, rows, row_chunk):
        r1 = min(r0 + row_chunk, rows)
        h_scr[r0:r1, :] = jnp.dot(ucat(r0, r1).astype(bf16), p_ref[...], preferred_element_type=f32)

    def step(k, carry):
        hr, hi = carry
        inr = h_scr[pl.ds(k, 1), :sb]
        ini = h_scr[pl.ds(k, 1), sb:]
        h_scr[pl.ds(k, 1), :sb] = hr
        h_scr[pl.ds(k, 1), sb:] = hi
        return lr * hr - li * hi + inr, lr * hi + li * hr + ini

    zero = jnp.zeros((1, sb), f32)
    hr, hi = lax.fori_loop(0, n_chunks, step, (zero, zero))
    fpr_ref[...] = hr
    fpi_ref[...] = hi
    h0r, h0i = h0r_ref[...], h0i_ref[...]
    fsr_ref[...] = lr * h0r - li * h0i + h_scr[n_chunks:, :sb]
    fsi_ref[...] = lr * h0i + li * h0r + h_scr[n_chunks:, sb:]
    h_scr[n_chunks:, :sb] = h0r
    h_scr[n_chunks:, sb:] = h0i

    dsk = dsk_ref[...]
    for r0 in range(0, rows, row_chunk):
        r1 = min(r0 + row_chunk, rows)
        u = ucat(r0, r1)
        y = jnp.dot(u.astype(bf16), w_ref[...], preferred_element_type=f32)
        y = y + jnp.dot(h_scr[r0:r1, :].astype(bf16), q_ref[...], preferred_element_type=f32)
        gl = jax.nn.gelu(y + dsk * u)
        for t in range(SSM_T):
            gl_ref[pl.ds(SSM_T * r0 + t, r1 - r0, stride=SSM_T), :] = gl[:, t * LANES:(t + 1) * LANES]


def s5_mixer(u, ops, d_skip, h0_re, h0_im, n_prompt):
    w, pm, qm, lr, li = ops
    m, dssm = u.shape
    nb = dssm // LANES
    t = SSM_T
    rows = m // t
    n_chunks = n_prompt // t
    nbat = h0_re.shape[0]
    assert rows - n_chunks == nbat and n_prompt % t == 0
    sb = STATE_BLOCK
    dsk = jnp.tile(d_skip.astype(f32).reshape(nb, 1, LANES), (1, 1, t))
    h0r = h0_re.astype(f32).reshape(nbat, nb * sb)
    h0i = h0_im.astype(f32).reshape(nbat, nb * sb)
    blk = lambda shape: pl.BlockSpec((None,) + shape, lambda i: (i, 0, 0))
    ublk = pl.BlockSpec((m, LANES), lambda i: (0, i))
    hblk = pl.BlockSpec((nbat, sb), lambda i: (0, i))
    fblk = pl.BlockSpec((1, sb), lambda i: (0, i))
    shp = jax.ShapeDtypeStruct
    return pl.pallas_call(
        functools.partial(_s5_kernel, n_chunks=n_chunks, row_chunk=384),
        out_shape=[shp((m, dssm), f32), shp((1, nb * sb), f32), shp((1, nb * sb), f32),
                   shp((nbat, nb * sb), f32), shp((nbat, nb * sb), f32)],
        grid=(nb,),
        in_specs=[ublk, blk((t * LANES, t * LANES)), blk((t * LANES, 2 * sb)), blk((2 * sb, t * LANES)),
                  blk((1, sb)), blk((1, sb)), blk((1, t * LANES)), hblk, hblk],
        out_specs=[ublk, fblk, fblk, hblk, hblk],
        scratch_shapes=[pltpu.VMEM((rows, 2 * sb), f32)],
        compiler_params=_cparams(("parallel",), 56),
        name="s5_mixer",
    )(u, w, pm, qm, lr, li, dsk, h0r, h0i)


def _cross_attn_kernel(q_ref, k_ref, v_ref, o_ref, *, scale):
    q = q_ref[...].astype(bf16)
    for h in range(MEM_HEADS):
        hs = slice(h * MEM_HEAD_DIM, (h + 1) * MEM_HEAD_DIM)
        if len(k_ref.shape) == 4:
            k = k_ref[:, :, h, :].astype(bf16)
            v = v_ref[:, :, h, :].astype(bf16)
        else:
            k = k_ref[:, :, hs].astype(bf16)
            v = v_ref[:, :, hs].astype(bf16)
        s = jnp.einsum("gqd,gkd->gqk", q[:, :, hs], k, preferred_element_type=f32) * scale
        m = jnp.max(s, axis=-1, keepdims=True)
        p = jnp.exp(s - m)
        p = p / jnp.sum(p, axis=-1, keepdims=True)
        o = jnp.einsum("gqk,gkd->gqd", p.astype(bf16), v, preferred_element_type=f32)
        o_ref[:, :, hs] = o.astype(o_ref.dtype)


def cross_attention(q3, k, v, nq, gb, tq, q_block0, name):
    g = k.shape[0]
    e = q3.shape[2]
    kblock = (gb,) + k.shape[1:]
    kmap = (lambda i, j: (i, 0, 0)) if k.ndim == 3 else (lambda i, j: (i, 0, 0, 0))
    return pl.pallas_call(
        functools.partial(_cross_attn_kernel, scale=1.0 / math.sqrt(MEM_HEAD_DIM)),
        out_shape=jax.ShapeDtypeStruct((g, nq, e), bf16),
        grid=(g // gb, nq // tq),
        in_specs=[pl.BlockSpec((gb, tq, e), lambda i, j: (q_block0 + i, j, 0)),
                  pl.BlockSpec(kblock, kmap), pl.BlockSpec(kblock, kmap)],
        out_specs=pl.BlockSpec((gb, tq, e), lambda i, j: (i, j, 0)),
        compiler_params=_cparams(("parallel", "parallel"), 48),
        name=name,
    )(q3, k, v)


def _rotary_tables(pos):
    inv = 1.0 / (ROPE_THETA ** (jnp.arange(0, HEAD_DIM, 2, dtype=f32) / HEAD_DIM))
    ang = pos.astype(f32)[:, None] * inv[None, :]
    cos, sin = jnp.cos(ang), jnp.sin(ang)
    return jnp.concatenate([cos, cos], axis=-1), jnp.concatenate([-sin, sin], axis=-1)


def kernel(x_prompt, x_sample, cache_win_k, cache_win_v, state_ssm_re, state_ssm_im, cache_mem_k, cache_mem_v,
           mem_prompt, g_mix, w_in, a_re, a_im, log_dt, b_re, b_im, c_re, c_im, d_skip, w_glu, b_glu,
           g_attn_out, g_ssm_out, w_out, g_cross, g_mem, w_mq, w_mk, w_mv, w_mo, g_ffn, w_up, w_down, g_final):
    bp, sp, dm = x_prompt.shape
    nbat, dec, _ = x_sample.shape
    assert bp == 1
    ns = nbat * dec
    n_groups = a_re.shape[0]
    d_ssm = n_groups * SSM_GROUP
    d_attn = dm - d_ssm
    n_heads = d_attn // HEAD_DIM
    e_mem = MEM_HEADS * MEM_HEAD_DIM
    n_mem = mem_prompt.shape[1]
    keep = min(PAST_LEN, sp)

    xp, xs = x_prompt.reshape(sp, dm), x_sample.reshape(ns, dm)
    mtot = sp + ns
    pos = jnp.concatenate([jnp.arange(sp, dtype=jnp.int32), PAST_LEN + jnp.tile(jnp.arange(dec, dtype=jnp.int32), nbat)])
    cos2, sin2 = _rotary_tables(pos)
    wb = lambda w_: w_.astype(bf16)

    h = rmsnorm2(xp, xs, g_mix, bf16)
    rot =((cos2, "rowtab"), (sin2, "rowtab"))
    w_in_b = wb(w_in)
    q = matmul(h, w_in_b, _ep_rotary, rot, head_major=True, cols=(0, d_attn), name="proj_q")
    k = matmul(h, w_in_b, _ep_rotary, rot, head_major=True, cols=(d_attn, d_attn), name="proj_k")
    v = matmul(h, w_in_b, head_major=True, cols=(2 * d_attn, d_attn), name="proj_v")
    u = matmul(h, w_in_b, cols=(3 * d_attn, d_ssm), name="proj_u")

    attn_p = prompt_attention(q, k, v, sp)
    attn_s = sample_attention(q, k, v, cache_win_k, cache_win_v, sp)
    na = rmsnorm2(attn_p, attn_s, g_attn_out, bf16)

    ops = s5_chunk_operators(a_re, a_im, log_dt, b_re, b_im, c_re, c_im)
    gl, fpr, fpi, fsr, fsi = s5_mixer(u, ops, d_skip, state_ssm_re, state_ssm_im, sp)
    y_ssm = matmul(gl, wb(w_glu), _ep_glu, ((gl, "tile"), (b_glu.astype(f32).reshape(1, d_ssm), "col")), name="glu")
    ny = rmsnorm(y_ssm, g_ssm_out, bf16)

    x1 = out_projection(na, ny, wb(w_out[:d_attn]), wb(w_out[d_attn:]), xp, xs)

    mem_n = rmsnorm(mem_prompt.reshape(n_mem, dm), g_mem, bf16)
    mem_k_p = matmul(mem_n, wb(w_mk), name="mem_k")
    mem_v_p = matmul(mem_n, wb(w_mv), name="mem_v")
    hc = rmsnorm(x1, g_cross, bf16)
    qm = matmul(hc, wb(w_mq), name="mem_q")
    o_p = cross_attention(qm.reshape(1, mtot, e_mem), mem_k_p.reshape(1, n_mem, e_mem), mem_v_p.reshape(1, n_mem, e_mem),
                          sp, 1, 512, 0, "cross_attn_prompt")
    o_s = cross_attention(qm.reshape(mtot // dec, dec, e_mem), cache_mem_k, cache_mem_v,
                          dec, 8, dec, sp // dec // 8, "cross_attn_sample")
    o_c = jnp.concatenate([o_p.reshape(sp, e_mem), o_s.reshape(ns, e_mem)], axis=0)
    x2 = matmul(o_c, wb(w_mo), _ep_residual, ((x1, "tile"),), name="mem_out")

    hf = rmsnorm(x2, g_ffn, bf16)
    hid = matmul(hf, wb(w_up), _ep_relu2, out_dtype=bf16, name="ffn_up")
    x3 = matmul(hid, wb(w_down), _ep_residual, ((x2, "tile"),), name="ffn_down")
    y_p = rmsnorm(x3, g_final, f32, 0, sp)
    y_s = rmsnorm(x3, g_final, f32, sp, ns)

    hd = (n_heads, HEAD_DIM)
    natural = lambda t, r0, r1: t[:, r0:r1].transpose(1, 0, 2)
    return (y_p.reshape(1, sp, dm), y_s.reshape(nbat, dec, dm),
            natural(k, sp - keep, sp).reshape(1, keep, *hd), natural(v, sp - keep, sp).reshape(1, keep, *hd),
            fpr.reshape(1, n_groups, SSM_STATE), fpi.reshape(1, n_groups, SSM_STATE),
            mem_k_p.reshape(1, n_mem, MEM_HEADS, MEM_HEAD_DIM), mem_v_p.reshape(1, n_mem, MEM_HEADS, MEM_HEAD_DIM),
            natural(k, sp, mtot).reshape(nbat, dec, *hd), natural(v, sp, mtot).reshape(nbat, dec, *hd),
            fsr.reshape(nbat, n_groups, SSM_STATE), fsi.reshape(nbat, n_groups, SSM_STATE))
```

```python
import functools
import math

import numpy as np
import jax
import jax.numpy as jnp
from jax import lax
from jax.experimental import pallas as pl
from jax.experimental.pallas import tpu as pltpu

f32 = jnp.float32
bf16 = jnp.bfloat16

HEAD_DIM = 128
SSM_GROUP = 16
SSM_STATE = 64
DILATED_BRANCHES = ((128, 1), (512, 4), (2048, 16))
MAX_DIL = 16
PAST_LEN = 2048
MEM_HEADS = 4
MEM_HEAD_DIM = 128
ROPE_THETA = 10000.0
NORM_EPS = 1e-6
NEG_BIG = -1e30

LANES = 128
SUBLANES = 8
SSM_T = 8
GROUPS_PER_BLOCK = LANES // SSM_GROUP
STATE_BLOCK = GROUPS_PER_BLOCK * SSM_STATE
ATTN_QB = 128
S5_SEG = 128
S5_SEG_PITCH = S5_SEG + SUBLANES
V7X_VMEM_BYTES = 64 * 1024 * 1024


def _cparams(sem, vmem_mb):
    assert vmem_mb * 1024 * 1024 < V7X_VMEM_BYTES
    return pltpu.CompilerParams(dimension_semantics=sem, vmem_limit_bytes=vmem_mb * 1024 * 1024)


def _rmsnorm_kernel(x_ref, g_ref, o_ref):
    x = x_ref[...].astype(f32)
    y = x * lax.rsqrt(jnp.mean(x * x, axis=-1, keepdims=True) + NORM_EPS)
    o_ref[...] = (y * g_ref[...]).astype(o_ref.dtype)


def rmsnorm(x, g, out_dtype, row_start=0, rows=None, tm=256):
    m, d = x.shape
    rows = m if rows is None else rows
    tm = min(tm, rows)
    assert rows % tm == 0 and row_start % tm == 0
    off = row_start // tm
    return pl.pallas_call(
        _rmsnorm_kernel,
        out_shape=jax.ShapeDtypeStruct((rows, d), out_dtype),
        grid=(rows // tm,),
        in_specs=[pl.BlockSpec((tm, d), lambda i: (i + off, 0)),
                  pl.BlockSpec((1, d), lambda i: (0, 0))],
        out_specs=pl.BlockSpec((tm, d), lambda i: (i, 0)),
        compiler_params=_cparams(("parallel",), 40),
        name="rmsnorm",
    )(x, g.reshape(1, d).astype(f32))


def _rmsnorm2_kernel(xa_ref, xb_ref, g_ref, o_ref, *, na):
    i = pl.program_id(0)

    @pl.when(i < na)
    def _():
        _rmsnorm_kernel(xa_ref, g_ref, o_ref)

    @pl.when(i >= na)
    def _():
        _rmsnorm_kernel(xb_ref, g_ref, o_ref)


def rmsnorm2(xa, xb, g, out_dtype, tm=256):
    (ma, d), (mb, _) = xa.shape, xb.shape
    assert ma % tm == 0 and mb % tm == 0
    na = ma // tm
    return pl.pallas_call(
        functools.partial(_rmsnorm2_kernel, na=na),
        out_shape=jax.ShapeDtypeStruct((ma + mb, d), out_dtype),
        grid=((ma + mb) // tm,),
        in_specs=[pl.BlockSpec((tm, d), lambda i: (jnp.minimum(i, na - 1), 0)),
                  pl.BlockSpec((tm, d), lambda i: (jnp.maximum(i - na, 0), 0)),
                  pl.BlockSpec((1, d), lambda i: (0, 0))],
        out_specs=pl.BlockSpec((tm, d), lambda i: (i, 0)),
        compiler_params=_cparams(("arbitrary",), 40),
        name="rmsnorm2",
    )(xa, xb, g.reshape(1, d).astype(f32))


def _ep_none(acc):
    return acc


def _ep_residual(acc, res):
    return acc + res


def _ep_relu2(acc):
    r = jnp.maximum(acc, 0.0)
    return r * r


def _ep_glu(acc, gl, bias):
    return gl * jax.nn.sigmoid(acc + bias)


def _ep_rotary(acc, cos2, sin2):
    parts = []
    for h in range(acc.shape[1] // HEAD_DIM):
        blk = acc[:, h * HEAD_DIM:(h + 1) * HEAD_DIM]
        parts.append(blk * cos2 + pltpu.roll(blk, HEAD_DIM // 2, 1) * sin2)
    return jnp.concatenate(parts, axis=-1)


def _mm_kernel(*refs, nk, epilogue, n_extra, head_major):
    a_ref, b_ref = refs[0], refs[1]
    extra = refs[2:2 + n_extra]
    o_ref = refs[2 + n_extra]
    part = jnp.dot(a_ref[...].astype(bf16), b_ref[...].astype(bf16), preferred_element_type=f32)

    def finish(acc):
        res = epilogue(acc, *[e[...] for e in extra]).astype(o_ref.dtype)
        if head_major:
            for h in range(o_ref.shape[0]):
                o_ref[h] = res[:, h * LANES:(h + 1) * LANES]
        else:
            o_ref[...] = res

    if nk == 1:
        finish(part)
        return
    acc_ref = o_ref
    k = pl.program_id(2)

    @pl.when(k == 0)
    def _():
        acc_ref[...] = part

    @pl.when((k > 0) & (k < nk - 1))
    def _():
        acc_ref[...] += part

    @pl.when(k == nk - 1)
    def _():
        finish(acc_ref[...] + part)


def _largest_tile(n, cap):
    best = None
    for t in range(LANES, cap + 1, LANES):
        if n % t == 0:
            best = t
    assert best is not None, (n, cap)
    return best


def matmul(a, b, epilogue=_ep_none, extras=(), out_dtype=f32, head_major=False, cols=None, tn_cap=1024,
           name="matmul"):
    m, kdim = a.shape
    col0, n = (0, b.shape[1]) if cols is None else cols
    tm = 1024 if m % 1024 == 0 else m
    tn = _largest_tile(math.gcd(n, col0) if col0 else n, tn_cap)
    tk = kdim if kdim <= 4096 else 2048
    assert m % tm == 0 and kdim % tk == 0 and col0 % tn == 0 and n % tn == 0
    nk = kdim // tk
    assert nk == 1 or (out_dtype == f32 and not head_major)
    joff = col0 // tn
    in_specs = [pl.BlockSpec((tm, tk), lambda i, j, k: (i, k)),
                pl.BlockSpec((tk, tn), lambda i, j, k: (k, j + joff))]
    ops = [a, b]
    for arr, kind in extras:
        if kind == "tile":
            in_specs.append(pl.BlockSpec((tm, tn), lambda i, j, k: (i, j)))
        elif kind == "col":
            in_specs.append(pl.BlockSpec((1, tn), lambda i, j, k: (0, j)))
        else:
            in_specs.append(pl.BlockSpec((tm, arr.shape[1]), lambda i, j, k: (i, 0)))
        ops.append(arr)
    if head_major:
        out_shape = jax.ShapeDtypeStruct((n // LANES, m, LANES), out_dtype)
        out_spec = pl.BlockSpec((tn // LANES, tm, LANES), lambda i, j, k: (j, i, 0))
    else:
        out_shape = jax.ShapeDtypeStruct((m, n), out_dtype)
        out_spec = pl.BlockSpec((tm, tn), lambda i, j, k: (i, j))
    return pl.pallas_call(
        functools.partial(_mm_kernel, nk=nk, epilogue=epilogue, n_extra=len(extras), head_major=head_major),
        out_shape=out_shape,
        grid=(m // tm, n // tn, nk),
        in_specs=in_specs,
        out_specs=out_spec,
        compiler_params=_cparams(("parallel", "parallel", "arbitrary"), 60),
        name=name,
    )(*ops)


def _out_proj_kernel(na_ref, ny_ref, wa_ref, wy_ref, xa_ref, xb_ref, o_ref, *, nblk_a):
    i = pl.program_id(0)
    acc = jnp.dot(na_ref[...], wa_ref[...], preferred_element_type=f32)
    acc = acc + jnp.dot(ny_ref[...], wy_ref[...], preferred_element_type=f32)

    @pl.when(i < nblk_a)
    def _():
        o_ref[...] = acc + xa_ref[...]

    @pl.when(i >= nblk_a)
    def _():
        o_ref[...] = acc + xb_ref[...]


def out_projection(na, ny, w_a, w_y, xa, xb):
    m, ka = na.shape
    ky = ny.shape[1]
    n = w_a.shape[1]
    tm = 1024
    tn = _largest_tile(n, 1024)
    assert m % tm == 0 and xa.shape[0] % tm == 0 and xa.shape[0] + xb.shape[0] == m
    nblk_a = xa.shape[0] // tm
    last_j = n // tn - 1
    xa_spec = pl.BlockSpec((tm, tn), lambda i, j: (jnp.minimum(i, nblk_a - 1), jnp.where(i < nblk_a, j, last_j)))
    xb_spec = pl.BlockSpec((tm, tn), lambda i, j: (jnp.maximum(i - nblk_a, 0), jnp.where(i < nblk_a, 0, j)))
    return pl.pallas_call(
        functools.partial(_out_proj_kernel, nblk_a=nblk_a),
        out_shape=jax.ShapeDtypeStruct((m, n), f32),
        grid=(m // tm, n // tn),
        in_specs=[pl.BlockSpec((tm, ka), lambda i, j: (i, 0)), pl.BlockSpec((tm, ky), lambda i, j: (i, 0)),
                  pl.BlockSpec((ka, tn), lambda i, j: (0, j)), pl.BlockSpec((ky, tn), lambda i, j: (0, j)),
                  xa_spec, xb_spec],
        out_specs=pl.BlockSpec((tm, tn), lambda i, j: (i, j)),
        compiler_params=_cparams(("arbitrary", "arbitrary"), 60),
        name="out_proj",
    )(na, ny, w_a, w_y, xa, xb)


def _norm_rows(x, g):
    return x * lax.rsqrt(jnp.mean(x * x, axis=-1, keepdims=True) + NORM_EPS) * g


def _norm_proj_kernel(x_ref, g_ref, w_ref, o_ref):
    h = _norm_rows(x_ref[...], g_ref[...]).astype(bf16)
    o_ref[...] = jnp.dot(h, w_ref[...], preferred_element_type=f32)


def norm_projection(x, g, w):
    m, d = x.shape
    n = w.shape[1]
    tm = 512
    assert m % tm == 0
    return pl.pallas_call(
        _norm_proj_kernel,
        out_shape=jax.ShapeDtypeStruct((m, n), f32),
        grid=(m // tm,),
        in_specs=[pl.BlockSpec((tm, d), lambda i: (i, 0)), pl.BlockSpec((1, d), lambda i: (0, 0)),
                  pl.BlockSpec((d, n), lambda i: (0, 0))],
        out_specs=pl.BlockSpec((tm, n), lambda i: (i, 0)),
        compiler_params=_cparams(("parallel",), 48),
        name="norm_proj",
    )(x, g.reshape(1, d).astype(f32), w)


def _proj_norm_kernel(a_ref, w_ref, res_ref, g_ref, x_ref, h_ref):
    x = res_ref[...] + jnp.dot(a_ref[...], w_ref[...], preferred_element_type=f32)
    x_ref[...] = x
    h_ref[...] = _norm_rows(x, g_ref[...]).astype(h_ref.dtype)


def projection_norm(a, w, res, g):
    m, kdim = a.shape
    n = w.shape[1]
    tm = 512
    assert m % tm == 0
    return pl.pallas_call(
        _proj_norm_kernel,
        out_shape=[jax.ShapeDtypeStruct((m, n), f32), jax.ShapeDtypeStruct((m, n), bf16)],
        grid=(m // tm,),
        in_specs=[pl.BlockSpec((tm, kdim), lambda i: (i, 0)), pl.BlockSpec((kdim, n), lambda i: (0, 0)),
                  pl.BlockSpec((tm, n), lambda i: (i, 0)), pl.BlockSpec((1, n), lambda i: (0, 0))],
        out_specs=[pl.BlockSpec((tm, n), lambda i: (i, 0)), pl.BlockSpec((tm, n), lambda i: (i, 0))],
        compiler_params=_cparams(("parallel",), 56),
        name="proj_norm",
    )(a, w, res, g.reshape(1, n).astype(f32))


def _prompt_attn_kernel(q_ref, kp_ref, kc_ref, vp_ref, vc_ref, o_ref, kk, vv, ob, lb, *, scale):
    c = pl.program_id(1)
    hg, blk, _ = q_ref.shape
    qb = ATTN_QB
    rows = lax.broadcasted_iota(jnp.int32, (qb, 2 * qb), 0)
    cols = lax.broadcasted_iota(jnp.int32, (qb, 2 * qb), 1)
    diff = qb + rows - cols
    band = (diff >= 0) & (diff <= qb)
    for h in range(hg):
        kk[0:blk, :] = kp_ref[h]
        kk[blk:2 * blk, :] = kc_ref[h]
        vv[0:blk, :] = vp_ref[h]
        vv[blk:2 * blk, :] = vc_ref[h]
        for g, (_, dil) in enumerate(DILATED_BRANCHES):

            def tile(t, carry, g=g, dil=dil):
                r = t % dil
                a = t // dil
                start = r + dil * qb * a
                q = q_ref[h, pl.ds(start, qb, stride=dil), :].astype(bf16)
                kstart = blk + start - dil * qb
                k = kk[pl.ds(kstart, 2 * qb, stride=dil), :].astype(bf16)
                v = vv[pl.ds(kstart, 2 * qb, stride=dil), :].astype(bf16)
                s = lax.dot_general(q, k, (((1,), (1,)), ((), ())), preferred_element_type=f32) * scale
                first_col = jnp.where((a > 0) | (c > 0), 0, qb)
                s = jnp.where(band & (cols >= first_col), s, NEG_BIG)
                m = jnp.max(s, axis=-1, keepdims=True)
                p = jnp.exp(s - m)
                l = jnp.sum(p, axis=-1, keepdims=True)
                o = jnp.dot(p.astype(bf16), v, preferred_element_type=f32) / l
                ob[g, pl.ds(start, qb, stride=dil), :] = o
                lb[g, pl.ds(start, qb, stride=dil), :] = jnp.broadcast_to(m + jnp.log(l), (qb, LANES))
                return carry

            lax.fori_loop(0, blk // qb, tile, 0, unroll=8)
        step = 256
        for r0 in range(0, blk, step):
            ls = [lb[g, r0:r0 + step, :] for g in range(len(DILATED_BRANCHES))]
            mx = functools.reduce(jnp.maximum, ls)
            ws = [jnp.exp(x - mx) for x in ls]
            tot = functools.reduce(lambda x, y: x + y, ws)
            acc = ws[0] * ob[0, r0:r0 + step, :]
            for g in range(1, len(DILATED_BRANCHES)):
                acc = acc + ws[g] * ob[g, r0:r0 + step, :]
            o_ref[r0:r0 + step, h * HEAD_DIM:(h + 1) * HEAD_DIM] = acc / tot


def prompt_attention(q, k, v, seq):
    n_heads, m, _ = q.shape
    blk = MAX_DIL * ATTN_QB
    hg = 2
    assert seq % blk == 0 and n_heads % hg == 0 and m >= seq
    assert all(w == dil * ATTN_QB for w, dil in DILATED_BRANCHES)
    cur = pl.BlockSpec((hg, blk, HEAD_DIM), lambda i, c: (i, c, 0))
    prev = pl.BlockSpec((hg, blk, HEAD_DIM), lambda i, c: (i, jnp.maximum(c - 1, 0), 0))
    nbr = len(DILATED_BRANCHES)
    return pl.pallas_call(
        functools.partial(_prompt_attn_kernel, scale=1.0 / math.sqrt(HEAD_DIM)),
        out_shape=jax.ShapeDtypeStruct((seq, n_heads * HEAD_DIM), f32),
        grid=(n_heads // hg, seq // blk),
        in_specs=[cur, prev, cur, prev, cur],
        out_specs=pl.BlockSpec((blk, hg * HEAD_DIM), lambda i, c: (c, i)),
        scratch_shapes=[pltpu.VMEM((2 * blk, HEAD_DIM), f32), pltpu.VMEM((2 * blk, HEAD_DIM), f32),
                        pltpu.VMEM((nbr, blk, HEAD_DIM), f32), pltpu.VMEM((nbr, blk, LANES), f32)],
        compiler_params=_cparams(("parallel", "arbitrary"), 48),
        name="prompt_attn",
    )(q, k, k, v, v)


def _sample_key_positions(win_len, dec):
    half = MAX_DIL // 2
    ma, ra = np.meshgrid(np.arange(win_len // MAX_DIL), np.arange(half), indexing="ij")
    pos_a = (MAX_DIL * ma + ra).reshape(-1)
    mb0 = 3 * (win_len // MAX_DIL) // 4
    mb, rb = np.meshgrid(np.arange(mb0, win_len // MAX_DIL), np.arange(half, MAX_DIL), indexing="ij")
    pos_b = (MAX_DIL * mb + rb).reshape(-1)
    return np.concatenate([pos_a, pos_b, win_len + np.arange(dec)])


def _sample_bias(win_len, dec, n_heads):
    kpos = _sample_key_positions(win_len, dec)[:, None]
    qpos = PAST_LEN + (np.arange(LANES) % dec)[None, :]
    delta = qpos - kpos
    out = []
    for window, dil in DILATED_BRANCHES:
        ok = (delta >= 0) & (delta <= window) & (delta % dil == 0) & (kpos >= 0)
        out.append(np.where(ok, 0.0, NEG_BIG))
    out = np.stack(out).astype(np.float32)
    out[:, :, n_heads * dec:] = NEG_BIG
    return out


def _sample_attn_kernel(q_ref, kn_ref, vn_ref, ka_ref, kb_ref, va_ref, vb_ref, bias_ref, o_ref, *, scale):
    n_heads, dec, _ = q_ref.shape
    d = n_heads * HEAD_DIM
    nq = LANES

    def gather(ref):
        n = ref.shape[1] * ref.shape[2]
        return jnp.concatenate([ref[h].reshape(n, HEAD_DIM) for h in range(n_heads)], axis=-1).astype(bf16)

    def natural(ref):
        return jnp.concatenate([ref[h] for h in range(n_heads)], axis=-1)

    qt = jnp.concatenate([natural(q_ref)] * (nq // dec), axis=0)
    rowh = lax.broadcasted_iota(jnp.int32, (nq, d), 0) // dec
    colh = lax.broadcasted_iota(jnp.int32, (nq, d), 1) // HEAD_DIM
    qbd = jnp.where(rowh == colh, qt, 0.0).astype(bf16)
    nt = (((1,), (1,)), ((), ()))
    sa = lax.dot_general(gather(ka_ref), qbd, nt, preferred_element_type=f32)
    sb = lax.dot_general(gather(kb_ref), qbd, nt, preferred_element_type=f32)
    sn = lax.dot_general(natural(kn_ref).astype(bf16), qbd, nt, preferred_element_type=f32)
    na, nb = sa.shape[0], sb.shape[0]
    s = jnp.concatenate([sa, sb, sn], axis=0) * scale
    ps, ls, lses = [], [], []
    for g in range(len(DILATED_BRANCHES)):
        sg = s + bias_ref[g]
        m = jnp.max(sg, axis=0, keepdims=True)
        p = jnp.exp(sg - m)
        l = jnp.sum(p, axis=0, keepdims=True)
        ps.append(p)
        ls.append(l)
        lses.append(m + jnp.log(l))
    mx = functools.reduce(jnp.maximum, lses)
    ws = [jnp.exp(x - mx) for x in lses]
    tot = functools.reduce(lambda x, y: x + y, ws)
    pm = ps[0] * (ws[0] / (tot * ls[0]))
    for p, w, l in zip(ps[1:], ws[1:], ls[1:]):
        pm = pm + p * (w / (tot * l))
    pm = pm.astype(bf16)
    tn = (((0,), (0,)), ((), ()))
    o = lax.dot_general(pm[:na], gather(va_ref), tn, preferred_element_type=f32)
    o = o + lax.dot_general(pm[na:na + nb], gather(vb_ref), tn, preferred_element_type=f32)
    o = o + lax.dot_general(pm[na + nb:], natural(vn_ref).astype(bf16), tn, preferred_element_type=f32)
    for h in range(n_heads):
        hs = slice(h * HEAD_DIM, (h + 1) * HEAD_DIM)
        o_ref[:, hs] = o[h * dec:(h + 1) * dec, hs]


def sample_attention(q, k, v, cache_k, cache_v, row0):
    n_heads, m, _ = q.shape
    nbat, win_len = cache_k.shape[0], cache_k.shape[1]
    dec = (m - row0) // nbat
    assert win_len == PAST_LEN and dec == SUBLANES and row0 % dec == 0 and n_heads * dec <= LANES
    assert win_len % (4 * MAX_DIL) == 0 and DILATED_BRANCHES[1][0] <= win_len // 4
    half = MAX_DIL // 2
    mgrp = win_len // MAX_DIL
    d = n_heads * HEAD_DIM
    ck = cache_k.transpose(0, 2, 1, 3).reshape(nbat, n_heads, mgrp, MAX_DIL, HEAD_DIM)
    cv = cache_v.transpose(0, 2, 1, 3).reshape(nbat, n_heads, mgrp, MAX_DIL, HEAD_DIM)
    bias = jnp.asarray(_sample_bias(win_len, dec, n_heads))
    new = pl.BlockSpec((n_heads, dec, HEAD_DIM), lambda b: (0, row0 // dec + b, 0))
    part_a = pl.BlockSpec((None, n_heads, mgrp, half, HEAD_DIM), lambda b: (b, 0, 0, 0, 0))
    part_b = pl.BlockSpec((None, n_heads, mgrp // 4, half, HEAD_DIM), lambda b: (b, 0, 3, 1, 0))
    out = pl.pallas_call(
        functools.partial(_sample_attn_kernel, scale=1.0 / math.sqrt(HEAD_DIM)),
        out_shape=jax.ShapeDtypeStruct((nbat, dec, d), f32),
        grid=(nbat,),
        in_specs=[new, new, new, part_a, part_b, part_a, part_b, pl.BlockSpec(bias.shape, lambda b: (0, 0, 0))],
        out_specs=pl.BlockSpec((None, dec, d), lambda b: (b, 0, 0)),
        compiler_params=_cparams(("parallel",), 56),
        name="sample_attn",
    )(q, k, v, ck, ck, cv, cv, bias)
    return out.reshape(nbat * dec, d)


def _s5_discretise(a_re, a_im, dt):
    mag = jnp.exp(a_re * dt)
    lam_re = mag * jnp.cos(a_im * dt)
    lam_im = mag * jnp.sin(a_im * dt)
    num_re = lam_re - 1.0
    num_im = lam_im
    den = a_re * a_re + a_im * a_im
    f_re = (num_re * a_re + num_im * a_im) / den
    f_im = (num_im * a_re - num_re * a_im) / den
    return lam_re, lam_im, f_re, f_im


def _cmul(ar, ai, br, bi):
    return ar * br - ai * bi, ar * bi + ai * br


def _s5_operators_kernel(arc, aic, ldc, btr, bti, ccr, cci, arb, aib, ldb, cbr, cbi, ars, ais, lds,
                         w_ref, p_ref, q_ref, lr_ref, li_ref):
    t_len = SSM_T
    nt = (((1,), (1,)), ((), ()))
    hi = lax.Precision.HIGHEST
    gsz, ssz = SSM_GROUP, SSM_STATE

    def group_mask(shape, row_div, col_div):
        r = lax.broadcasted_iota(jnp.int32, shape, 0) // row_div
        c = lax.broadcasted_iota(jnp.int32, shape, 1) // col_div
        return r == c

    lr, li, fr, fi = _s5_discretise(arc[...], aic[...], jnp.exp(ldc[...]))
    er, ei = _cmul(fr, fi, btr[...], bti[...])
    c_re, c_im = ccr[...], cci[...]
    mask_w = group_mask((LANES, LANES), gsz, gsz)
    mask_p = group_mask((LANES, STATE_BLOCK), gsz, ssz)
    zero_blk = jnp.zeros((LANES, LANES), bf16)
    d_blocks, e_pows = [], []
    for n in range(t_len):
        e_pows.append((er, ei))
        dn = (lax.dot_general(er, c_re, nt, precision=hi, preferred_element_type=f32)
              - lax.dot_general(ei, c_im, nt, precision=hi, preferred_element_type=f32))
        d_blocks.append(jnp.where(mask_w, dn, 0.0).astype(bf16))
        er, ei = _cmul(lr, li, er, ei)
    for ti in range(t_len):
        for to in range(t_len):
            w_ref[ti * LANES:(ti + 1) * LANES, to * LANES:(to + 1) * LANES] = (
                d_blocks[to - ti] if to >= ti else zero_blk)
        pr, pi = e_pows[t_len - 1 - ti]
        p_ref[ti * LANES:(ti + 1) * LANES, 0:STATE_BLOCK] = jnp.where(
            mask_p, jnp.concatenate([pr] * GROUPS_PER_BLOCK, axis=1), 0.0).astype(bf16)
        p_ref[ti * LANES:(ti + 1) * LANES, STATE_BLOCK:] = jnp.where(
            mask_p, jnp.concatenate([pi] * GROUPS_PER_BLOCK, axis=1), 0.0).astype(bf16)
    lbr, lbi, _, _ = _s5_discretise(arb[...], aib[...], jnp.exp(ldb[...]))
    mask_q = group_mask((STATE_BLOCK, LANES), ssz, gsz)
    qr, qi = cbr[...], cbi[...]
    for t in range(t_len):
        qr, qi = _cmul(qr, qi, lbr, lbi)
        q_ref[0:STATE_BLOCK, t * LANES:(t + 1) * LANES] = jnp.where(
            mask_q, jnp.concatenate([qr] * GROUPS_PER_BLOCK, axis=0), 0.0).astype(bf16)
        q_ref[STATE_BLOCK:, t * LANES:(t + 1) * LANES] = jnp.where(
            mask_q, jnp.concatenate([-qi] * GROUPS_PER_BLOCK, axis=0), 0.0).astype(bf16)
    lsr, lsi, _, _ = _s5_discretise(ars[...], ais[...], jnp.exp(lds[...]))
    pr, pi = lsr, lsi
    for _ in range(t_len - 1):
        pr, pi = _cmul(pr, pi, lsr, lsi)
    lr_ref[...] = pr
    li_ref[...] = pi


def s5_chunk_operators(a_re, a_im, log_dt, b_re, b_im, c_re, c_im):
    g, p = a_re.shape
    ch = b_re.shape[2]
    j = GROUPS_PER_BLOCK
    nb = g // j
    assert p == SSM_STATE and ch == SSM_GROUP and g % j == 0
    t = SSM_T
    a_re, a_im, log_dt, b_re, b_im, c_re, c_im = (x.astype(f32) for x in (a_re, a_im, log_dt, b_re, b_im, c_re, c_im))
    by_row = lambda x: jnp.repeat(x.reshape(nb, j, p), ch, axis=1)
    by_col = lambda x: jnp.repeat(x.reshape(nb, j, p).transpose(0, 2, 1), ch, axis=2)
    ld = log_dt.reshape(nb, j)
    ld_row = jnp.repeat(ld, ch, axis=1)[:, :, None]
    ld_col = jnp.repeat(ld, ch, axis=1)[:, None, :]
    ld_flat = jnp.repeat(ld, p, axis=1)[:, None, :]
    bt = lambda x: x.reshape(nb, j, p, ch).transpose(0, 1, 3, 2).reshape(nb, j * ch, p)
    c_row = lambda x: x.reshape(nb, j * ch, p)
    c_col = lambda x: x.reshape(nb, j, ch, p).transpose(0, 3, 1, 2).reshape(nb, p, j * ch)
    flat = lambda x: x.reshape(nb, 1, j * p)
    ops = [by_row(a_re), by_row(a_im), ld_row, bt(b_re), bt(b_im), c_row(c_re), c_row(c_im),
           by_col(a_re), by_col(a_im), ld_col, c_col(c_re), c_col(c_im),
           flat(a_re), flat(a_im), ld_flat]
    blk = lambda arr: pl.BlockSpec((None,) + arr.shape[1:], lambda i: (i, 0, 0))
    shp = jax.ShapeDtypeStruct
    sb = STATE_BLOCK
    outs = [shp((nb, t * LANES, t * LANES), bf16), shp((nb, t * LANES, 2 * sb), bf16),
            shp((nb, 2 * sb, t * LANES), bf16), shp((nb, 1, sb), f32), shp((nb, 1, sb), f32)]
    return pl.pallas_call(
        _s5_operators_kernel,
        out_shape=outs,
        grid=(nb,),
        in_specs=[blk(x) for x in ops],
        out_specs=[blk(x) for x in outs],
        compiler_params=_cparams(("parallel",), 32),
        name="s5_operators",
    )(*ops)


def _s5_kernel(u_ref, w_ref, p_ref, q_ref, lr_ref, li_ref, dsk_ref, h0r_ref, h0i_ref,
               gl_ref, fpr_ref, fpi_ref, fsr_ref, fsi_ref, h_scr, *, segs_per_dot):
    seg, pitch = S5_SEG, S5_SEG_PITCH
    n_seg = u_ref.shape[0] // (SSM_T * seg)
    n_pseg = n_seg - 1
    nsl = 2 * STATE_BLOCK // LANES
    half = nsl // 2
    lr, li = lr_ref[...], li_ref[...]
    slab = lambda x, l: x[:, l * LANES:(l + 1) * LANES]

    def ucat(s0):
        r0, n = s0 * seg, segs_per_dot * seg
        return jnp.concatenate([u_ref[pl.ds(SSM_T * r0 + t, n, stride=SSM_T), :] for t in range(SSM_T)], axis=-1)

    def seg_rows(s):
        return slice(s * pitch, s * pitch + seg)

    for s0 in range(0, n_seg, segs_per_dot):
        res = jnp.dot(ucat(s0).astype(bf16), p_ref[...], preferred_element_type=f32)
        for s in range(segs_per_dot):
            for l in range(nsl):
                h_scr[l, seg_rows(s0 + s), :] = slab(res[s * seg:(s + 1) * seg], l)

    lam_r = [jnp.broadcast_to(slab(lr, l), (SUBLANES, LANES)) for l in range(half)]
    lam_i = [jnp.broadcast_to(slab(li, l), (SUBLANES, LANES)) for l in range(half)]

    def load(k):
        return [h_scr[l, pl.ds(k, SUBLANES, stride=pitch), :] for l in range(nsl)]

    def advance(c, x, mr, mi):
        re = [mr[l] * c[l] - mi[l] * c[l + half] + x[l] for l in range(half)]
        im = [mr[l] * c[l + half] + mi[l] * c[l] + x[l + half] for l in range(half)]
        return tuple(re + im)

    zeros = tuple(jnp.zeros((SUBLANES, LANES), f32) for _ in range(nsl))
    ends = lax.fori_loop(0, seg, lambda k, c: advance(c, load(k), lam_r, lam_i), zeros, unroll=8)

    pr, pi = [slab(lr, l) for l in range(half)], [slab(li, l) for l in range(half)]
    for _ in range(seg.bit_length() - 1):
        sq = [_cmul(a, b, a, b) for a, b in zip(pr, pi)]
        pr, pi = [x[0] for x in sq], [x[1] for x in sq]
    cur = tuple(jnp.zeros((1, LANES), f32) for _ in range(nsl))
    starts = [cur]
    for j in range(n_pseg):
        cur = advance(cur, [e[j:j + 1, :] for e in ends], pr, pi)
        starts.append(cur)
    fpr_ref[...] = jnp.concatenate(starts[n_pseg][:half], axis=-1)
    fpi_ref[...] = jnp.concatenate(starts[n_pseg][half:], axis=-1)
    init = tuple(jnp.concatenate([starts[j][l] for j in range(n_pseg)], axis=0) for l in range(nsl))

    def sweep(k, c):
        x = load(k)
        for l in range(nsl):
            h_scr[l, pl.ds(k, SUBLANES, stride=pitch), :] = c[l]
        return advance(c, x, lam_r, lam_i)

    lax.fori_loop(0, seg, sweep, init, unroll=8)

    h0r, h0i = h0r_ref[...], h0i_ref[...]
    last = seg_rows(n_pseg)
    inr = jnp.concatenate([h_scr[l, last, :] for l in range(half)], axis=-1)
    ini = jnp.concatenate([h_scr[l + half, last, :] for l in range(half)], axis=-1)
    fsr_ref[...] = lr * h0r - li * h0i + inr
    fsi_ref[...] = lr * h0i + li * h0r + ini
    for l in range(half):
        h_scr[l, last, :] = slab(h0r, l)
        h_scr[l + half, last, :] = slab(h0i, l)

    dsk = dsk_ref[...]
    for s0 in range(0, n_seg, segs_per_dot):
        u = ucat(s0)
        hp = jnp.concatenate([jnp.concatenate([h_scr[l, seg_rows(s0 + s), :] for l in range(nsl)], axis=-1)
                              for s in range(segs_per_dot)], axis=0)
        y = jnp.dot(u.astype(bf16), w_ref[...], preferred_element_type=f32)
        y = y + jnp.dot(hp.astype(bf16), q_ref[...], preferred_element_type=f32)
        gl = jax.nn.gelu(y + dsk * u)
        r0, n = s0 * seg, segs_per_dot * seg
        for t in range(SSM_T):
            gl_ref[pl.ds(SSM_T * r0 + t, n, stride=SSM_T), :] = gl[:, t * LANES:(t + 1) * LANES]


def s5_mixer(u, ops, d_skip, h0_re, h0_im, n_prompt):
    w, pm, qm, lr, li = ops
    m, dssm = u.shape
    nb = dssm // LANES
    t = SSM_T
    nbat = h0_re.shape[0]
    n_seg = m // (t * S5_SEG)
    segs_per_dot = 3
    assert n_prompt == SUBLANES * S5_SEG * t and nbat == S5_SEG and m == n_prompt + nbat * t
    assert n_seg % segs_per_dot == 0
    sb = STATE_BLOCK
    dsk = jnp.tile(d_skip.astype(f32).reshape(nb, 1, LANES), (1, 1, t))
    h0r = h0_re.astype(f32).reshape(nbat, nb * sb)
    h0i = h0_im.astype(f32).reshape(nbat, nb * sb)
    blk = lambda shape: pl.BlockSpec((None,) + shape, lambda i: (i, 0, 0))
    ublk = pl.BlockSpec((m, LANES), lambda i: (0, i))
    hblk = pl.BlockSpec((nbat, sb), lambda i: (0, i))
    fblk = pl.BlockSpec((1, sb), lambda i: (0, i))
    shp = jax.ShapeDtypeStruct
    return pl.pallas_call(
        functools.partial(_s5_kernel, segs_per_dot=segs_per_dot),
        out_shape=[shp((m, dssm), f32), shp((1, nb * sb), f32), shp((1, nb * sb), f32),
                   shp((nbat, nb * sb), f32), shp((nbat, nb * sb), f32)],
        grid=(nb,),
        in_specs=[ublk, blk((t * LANES, t * LANES)), blk((t * LANES, 2 * sb)), blk((2 * sb, t * LANES)),
                  blk((1, sb)), blk((1, sb)), blk((1, t * LANES)), hblk, hblk],
        out_specs=[ublk, fblk, fblk, hblk, hblk],
        scratch_shapes=[pltpu.VMEM((2 * sb // LANES, n_seg * S5_SEG_PITCH, LANES), f32)],
        compiler_params=_cparams(("parallel",), 56),
        name="s5_mixer",
    )(u, w, pm, qm, lr, li, dsk, h0r, h0i)


def _cross_attn_cache_kernel(q_ref, k_ref, v_ref, o_ref, *, scale):
    gb, n, nh, hd = k_ref.shape
    dec = q_ref.shape[1]
    rowh = lax.broadcasted_iota(jnp.int32, (n * nh, nh * dec), 0) % nh
    colh = lax.broadcasted_iota(jnp.int32, (n * nh, nh * dec), 1) // dec
    same = rowh == colh
    for g in range(gb):
        kf = k_ref[g].reshape(n * nh, hd).astype(bf16)
        vf = v_ref[g].reshape(n * nh, hd).astype(bf16)
        q = q_ref[g]
        qs = jnp.concatenate([q[:, h * hd:(h + 1) * hd] for h in range(nh)], axis=0).astype(bf16)
        s = lax.dot_general(kf, qs, (((1,), (1,)), ((), ())), preferred_element_type=f32) * scale
        s = jnp.where(same, s, NEG_BIG)
        m = jnp.max(s, axis=0, keepdims=True)
        p = jnp.exp(s - m)
        p = p / jnp.sum(p, axis=0, keepdims=True)
        o = lax.dot_general(p.astype(bf16), vf, (((0,), (0,)), ((), ())), preferred_element_type=f32)
        for h in range(nh):
            o_ref[g, :, h * hd:(h + 1) * hd] = o[h * dec:(h + 1) * dec, :].astype(o_ref.dtype)


def _cross_attn_kernel(q_ref, k_ref, v_ref, o_ref, *, scale):
    q = q_ref[...].astype(bf16)
    for h in range(MEM_HEADS):
        hs = slice(h * MEM_HEAD_DIM, (h + 1) * MEM_HEAD_DIM)
        k = k_ref[:, :, hs].astype(bf16)
        v = v_ref[:, :, hs].astype(bf16)
        s = jnp.einsum("gqd,gkd->gqk", q[:, :, hs], k, preferred_element_type=f32) * scale
        m = jnp.max(s, axis=-1, keepdims=True)
        p = jnp.exp(s - m)
        p = p / jnp.sum(p, axis=-1, keepdims=True)
        o = jnp.einsum("gqk,gkd->gqd", p.astype(bf16), v, preferred_element_type=f32)
        o_ref[:, :, hs] = o.astype(o_ref.dtype)


def cross_attention(q3, k, v, nq, gb, tq, q_block0, name):
    g = k.shape[0]
    e = q3.shape[2]
    kblock = (gb,) + k.shape[1:]
    kmap = (lambda i, j: (i, 0, 0)) if k.ndim == 3 else (lambda i, j: (i, 0, 0, 0))
    body = _cross_attn_kernel if k.ndim == 3 else _cross_attn_cache_kernel
    return pl.pallas_call(
        functools.partial(body, scale=1.0 / math.sqrt(MEM_HEAD_DIM)),
        out_shape=jax.ShapeDtypeStruct((g, nq, e), bf16),
        grid=(g // gb, nq // tq),
        in_specs=[pl.BlockSpec((gb, tq, e), lambda i, j: (q_block0 + i, j, 0)),
                  pl.BlockSpec(kblock, kmap), pl.BlockSpec(kblock, kmap)],
        out_specs=pl.BlockSpec((gb, tq, e), lambda i, j: (i, j, 0)),
        compiler_params=_cparams(("parallel", "parallel"), 48),
        name=name,
    )(q3, k, v)


def _rotary_tables(pos):
    inv = 1.0 / (ROPE_THETA ** (jnp.arange(0, HEAD_DIM, 2, dtype=f32) / HEAD_DIM))
    ang = pos.astype(f32)[:, None] * inv[None, :]
    cos, sin = jnp.cos(ang), jnp.sin(ang)
    return jnp.concatenate([cos, cos], axis=-1), jnp.concatenate([-sin, sin], axis=-1)


def kernel(x_prompt, x_sample, cache_win_k, cache_win_v, state_ssm_re, state_ssm_im, cache_mem_k, cache_mem_v,
           mem_prompt, g_mix, w_in, a_re, a_im, log_dt, b_re, b_im, c_re, c_im, d_skip, w_glu, b_glu,
           g_attn_out, g_ssm_out, w_out, g_cross, g_mem, w_mq, w_mk, w_mv, w_mo, g_ffn, w_up, w_down, g_final):
    bp, sp, dm = x_prompt.shape
    nbat, dec, _ = x_sample.shape
    assert bp == 1
    ns = nbat * dec
    n_groups = a_re.shape[0]
    d_ssm = n_groups * SSM_GROUP
    d_attn = dm - d_ssm
    n_heads = d_attn // HEAD_DIM
    e_mem = MEM_HEADS * MEM_HEAD_DIM
    n_mem = mem_prompt.shape[1]
    keep = min(PAST_LEN, sp)

    xp, xs = x_prompt.reshape(sp, dm), x_sample.reshape(ns, dm)
    mtot = sp + ns
    pos = jnp.concatenate([jnp.arange(sp, dtype=jnp.int32), PAST_LEN + jnp.tile(jnp.arange(dec, dtype=jnp.int32), nbat)])
    cos2, sin2 = _rotary_tables(pos)
    wb = lambda w_: w_.astype(bf16)

    h = rmsnorm2(xp, xs, g_mix, bf16)
    rot =((cos2, "rowtab"), (sin2, "rowtab"))
    w_in_b = wb(w_in)
    q = matmul(h, w_in_b, _ep_rotary, rot, head_major=True, cols=(0, d_attn), name="proj_q")
    k = matmul(h, w_in_b, _ep_rotary, rot, head_major=True, cols=(d_attn, d_attn), name="proj_k")
    v = matmul(h, w_in_b, head_major=True, cols=(2 * d_attn, d_attn), name="proj_v")
    u = matmul(h, w_in_b, cols=(3 * d_attn, d_ssm), name="proj_u")

    attn_p = prompt_attention(q, k, v, sp)
    attn_s = sample_attention(q, k, v, cache_win_k, cache_win_v, sp)
    na = rmsnorm2(attn_p, attn_s, g_attn_out, bf16)

    ops = s5_chunk_operators(a_re, a_im, log_dt, b_re, b_im, c_re, c_im)
    gl, fpr, fpi, fsr, fsi = s5_mixer(u, ops, d_skip, state_ssm_re, state_ssm_im, sp)
    y_ssm = matmul(gl, wb(w_glu), _ep_glu, ((gl, "tile"), (b_glu.astype(f32).reshape(1, d_ssm), "col")), tn_cap=512,
                   name="glu")
    ny = rmsnorm(y_ssm, g_ssm_out, bf16)

    x1 = out_projection(na, ny, wb(w_out[:d_attn]), wb(w_out[d_attn:]), xp, xs)

    mem_n = rmsnorm(mem_prompt.reshape(n_mem, dm), g_mem, bf16)
    mem_k_p = matmul(mem_n, wb(w_mk), name="mem_k")
    mem_v_p = matmul(mem_n, wb(w_mv), name="mem_v")
    qm = norm_projection(x1, g_cross, wb(w_mq))
    o_p = cross_attention(qm.reshape(1, mtot, e_mem), mem_k_p.reshape(1, n_mem, e_mem), mem_v_p.reshape(1, n_mem, e_mem),
                          sp, 1, 512, 0, "cross_attn_prompt")
    o_s = cross_attention(qm.reshape(mtot // dec, dec, e_mem), cache_mem_k, cache_mem_v,
                          dec, 8, dec, sp // dec // 8, "cross_attn_sample")
    o_c = jnp.concatenate([o_p.reshape(sp, e_mem), o_s.reshape(ns, e_mem)], axis=0)
    x2, hf = projection_norm(o_c, wb(w_mo), x1, g_ffn)

    hid = matmul(hf, w_up, _ep_relu2, out_dtype=bf16, tn_cap=512, name="ffn_up")
    x3 = matmul(hid, wb(w_down), _ep_residual, ((x2, "tile"),), name="ffn_down")
    y_p = rmsnorm(x3, g_final, f32, 0, sp)
    y_s = rmsnorm(x3, g_final, f32, sp, ns)

    hd = (n_heads, HEAD_DIM)
    natural = lambda t, r0, r1: t[:, r0:r1].transpose(1, 0, 2)
    return (y_p.reshape(1, sp, dm), y_s.reshape(nbat, dec, dm),
            natural(k, sp - keep, sp).reshape(1, keep, *hd), natural(v, sp - keep, sp).reshape(1, keep, *hd),
            fpr.reshape(1, n_groups, SSM_STATE), fpi.reshape(1, n_groups, SSM_STATE),
            mem_k_p.reshape(1, n_mem, MEM_HEADS, MEM_HEAD_DIM), mem_v_p.reshape(1, n_mem, MEM_HEADS, MEM_HEAD_DIM),
            natural(k, sp, mtot).reshape(nbat, dec, *hd), natural(v, sp, mtot).reshape(nbat, dec, *hd),
            fsr.reshape(nbat, n_groups, SSM_STATE), fsi.reshape(nbat, n_groups, SSM_STATE))
```

```python
import functools
import math

import numpy as np
import jax
import jax.numpy as jnp
from jax import lax
from jax.experimental import pallas as pl
from jax.experimental.pallas import tpu as pltpu

f32 = jnp.float32
bf16 = jnp.bfloat16

HEAD_DIM = 128
SSM_GROUP = 16
SSM_STATE = 64
DILATED_BRANCHES = ((128, 1), (512, 4), (2048, 16))
MAX_DIL = 16
PAST_LEN = 2048
MEM_HEADS = 4
MEM_HEAD_DIM = 128
ROPE_THETA = 10000.0
NORM_EPS = 1e-6
NEG_BIG = -1e30
LOG2E = math.log2(math.e)
LN2 = math.log(2.0)

LANES = 128
SUBLANES = 8
SSM_T = 8
GROUPS_PER_BLOCK = LANES // SSM_GROUP
STATE_BLOCK = GROUPS_PER_BLOCK * SSM_STATE
ATTN_QB = 128
S5_SEG = 128
S5_SEG_PITCH = S5_SEG + SUBLANES
V7X_VMEM_BYTES = 64 * 1024 * 1024


def _cparams(sem, vmem_mb):
    assert vmem_mb * 1024 * 1024 < V7X_VMEM_BYTES
    return pltpu.CompilerParams(dimension_semantics=sem, vmem_limit_bytes=vmem_mb * 1024 * 1024)


def _rmsnorm_kernel(x_ref, g_ref, o_ref):
    x = x_ref[...].astype(f32)
    y = x * lax.rsqrt(jnp.mean(x * x, axis=-1, keepdims=True) + NORM_EPS)
    o_ref[...] = (y * g_ref[...]).astype(o_ref.dtype)


def rmsnorm(x, g, out_dtype, row_start=0, rows=None, tm=256):
    m, d = x.shape
    rows = m if rows is None else rows
    tm = min(tm, rows)
    assert rows % tm == 0 and row_start % tm == 0
    off = row_start // tm
    return pl.pallas_call(
        _rmsnorm_kernel,
        out_shape=jax.ShapeDtypeStruct((rows, d), out_dtype),
        grid=(rows // tm,),
        in_specs=[pl.BlockSpec((tm, d), lambda i: (i + off, 0)),
                  pl.BlockSpec((1, d), lambda i: (0, 0))],
        out_specs=pl.BlockSpec((tm, d), lambda i: (i, 0)),
        compiler_params=_cparams(("parallel",), 40),
        name="rmsnorm",
    )(x, g.reshape(1, d).astype(f32))


def _rmsnorm2_kernel(xa_ref, xb_ref, g_ref, o_ref, *, na):
    i = pl.program_id(0)

    @pl.when(i < na)
    def _():
        _rmsnorm_kernel(xa_ref, g_ref, o_ref)

    @pl.when(i >= na)
    def _():
        _rmsnorm_kernel(xb_ref, g_ref, o_ref)


def rmsnorm2(xa, xb, g, out_dtype, tm=256):
    (ma, d), (mb, _) = xa.shape, xb.shape
    assert ma % tm == 0 and mb % tm == 0
    na = ma // tm
    return pl.pallas_call(
        functools.partial(_rmsnorm2_kernel, na=na),
        out_shape=jax.ShapeDtypeStruct((ma + mb, d), out_dtype),
        grid=((ma + mb) // tm,),
        in_specs=[pl.BlockSpec((tm, d), lambda i: (jnp.minimum(i, na - 1), 0)),
                  pl.BlockSpec((tm, d), lambda i: (jnp.maximum(i - na, 0), 0)),
                  pl.BlockSpec((1, d), lambda i: (0, 0))],
        out_specs=pl.BlockSpec((tm, d), lambda i: (i, 0)),
        compiler_params=_cparams(("arbitrary",), 40),
        name="rmsnorm2",
    )(xa, xb, g.reshape(1, d).astype(f32))


def _ep_none(acc):
    return acc


def _ep_residual(acc, res):
    return acc + res


def _ep_relu2(acc):
    r = jnp.maximum(acc, 0.0)
    return r * r


def _ep_glu(acc, gl, bias):
    return gl * jax.nn.sigmoid(acc + bias)


def _ep_rotary(acc, cos2, sin2):
    parts = []
    for h in range(acc.shape[1] // HEAD_DIM):
        blk = acc[:, h * HEAD_DIM:(h + 1) * HEAD_DIM]
        parts.append(blk * cos2 + pltpu.roll(blk, HEAD_DIM // 2, 1) * sin2)
    return jnp.concatenate(parts, axis=-1)


def _mm_kernel(*refs, nk, epilogue, n_extra, head_major):
    a_ref, b_ref = refs[0], refs[1]
    extra = refs[2:2 + n_extra]
    o_ref = refs[2 + n_extra]
    part = jnp.dot(a_ref[...].astype(bf16), b_ref[...].astype(bf16), preferred_element_type=f32)

    def finish(acc):
        res = epilogue(acc, *[e[...] for e in extra]).astype(o_ref.dtype)
        if head_major:
            for h in range(o_ref.shape[0]):
                o_ref[h] = res[:, h * LANES:(h + 1) * LANES]
        else:
            o_ref[...] = res

    if nk == 1:
        finish(part)
        return
    acc_ref = o_ref
    k = pl.program_id(2)

    @pl.when(k == 0)
    def _():
        acc_ref[...] = part

    @pl.when((k > 0) & (k < nk - 1))
    def _():
        acc_ref[...] += part

    @pl.when(k == nk - 1)
    def _():
        finish(acc_ref[...] + part)


def _largest_tile(n, cap):
    best = None
    for t in range(LANES, cap + 1, LANES):
        if n % t == 0:
            best = t
    assert best is not None, (n, cap)
    return best


def matmul(a, b, epilogue=_ep_none, extras=(), out_dtype=f32, head_major=False, cols=None, tm_pref=1024,
           tn_cap=1024, name="matmul"):
    m, kdim = a.shape
    col0, n = (0, b.shape[1]) if cols is None else cols
    tm = next((t for t in (tm_pref, 1024) if m % t == 0), m)
    tn = _largest_tile(math.gcd(n, col0) if col0 else n, tn_cap)
    tk = kdim if kdim <= 4096 else 2048
    assert m % tm == 0 and kdim % tk == 0 and col0 % tn == 0 and n % tn == 0
    nk = kdim // tk
    assert nk == 1 or (out_dtype == f32 and not head_major)
    joff = col0 // tn
    in_specs = [pl.BlockSpec((tm, tk), lambda i, j, k: (i, k)),
                pl.BlockSpec((tk, tn), lambda i, j, k: (k, j + joff))]
    ops = [a, b]
    for arr, kind in extras:
        if kind == "tile":
            in_specs.append(pl.BlockSpec((tm, tn), lambda i, j, k: (i, j)))
        elif kind == "col":
            in_specs.append(pl.BlockSpec((1, tn), lambda i, j, k: (0, j)))
        else:
            in_specs.append(pl.BlockSpec((tm, arr.shape[1]), lambda i, j, k: (i, 0)))
        ops.append(arr)
    if head_major:
        out_shape = jax.ShapeDtypeStruct((n // LANES, m, LANES), out_dtype)
        out_spec = pl.BlockSpec((tn // LANES, tm, LANES), lambda i, j, k: (j, i, 0))
    else:
        out_shape = jax.ShapeDtypeStruct((m, n), out_dtype)
        out_spec = pl.BlockSpec((tm, tn), lambda i, j, k: (i, j))
    return pl.pallas_call(
        functools.partial(_mm_kernel, nk=nk, epilogue=epilogue, n_extra=len(extras), head_major=head_major),
        out_shape=out_shape,
        grid=(m // tm, n // tn, nk),
        in_specs=in_specs,
        out_specs=out_spec,
        compiler_params=_cparams(("parallel", "parallel", "arbitrary"), 60),
        name=name,
    )(*ops)


def _out_proj_kernel(na_ref, ny_ref, wa_ref, wy_ref, xa_ref, xb_ref, o_ref, *, nblk_a):
    i = pl.program_id(0)
    acc = jnp.dot(na_ref[...], wa_ref[...], preferred_element_type=f32)
    acc = acc + jnp.dot(ny_ref[...], wy_ref[...], preferred_element_type=f32)

    @pl.when(i < nblk_a)
    def _():
        o_ref[...] = acc + xa_ref[...]

    @pl.when(i >= nblk_a)
    def _():
        o_ref[...] = acc + xb_ref[...]


def out_projection(na, ny, w_a, w_y, xa, xb):
    m, ka = na.shape
    ky = ny.shape[1]
    n = w_a.shape[1]
    tm = 1024
    tn = _largest_tile(n, 1024)
    assert m % tm == 0 and xa.shape[0] % tm == 0 and xa.shape[0] + xb.shape[0] == m
    nblk_a = xa.shape[0] // tm
    last_j = n // tn - 1
    xa_spec = pl.BlockSpec((tm, tn), lambda i, j: (jnp.minimum(i, nblk_a - 1), jnp.where(i < nblk_a, j, last_j)))
    xb_spec = pl.BlockSpec((tm, tn), lambda i, j: (jnp.maximum(i - nblk_a, 0), jnp.where(i < nblk_a, 0, j)))
    return pl.pallas_call(
        functools.partial(_out_proj_kernel, nblk_a=nblk_a),
        out_shape=jax.ShapeDtypeStruct((m, n), f32),
        grid=(m // tm, n // tn),
        in_specs=[pl.BlockSpec((tm, ka), lambda i, j: (i, 0)), pl.BlockSpec((tm, ky), lambda i, j: (i, 0)),
                  pl.BlockSpec((ka, tn), lambda i, j: (0, j)), pl.BlockSpec((ky, tn), lambda i, j: (0, j)),
                  xa_spec, xb_spec],
        out_specs=pl.BlockSpec((tm, tn), lambda i, j: (i, j)),
        compiler_params=_cparams(("arbitrary", "arbitrary"), 60),
        name="out_proj",
    )(na, ny, w_a, w_y, xa, xb)


def _norm_rows(x, g):
    return x * lax.rsqrt(jnp.mean(x * x, axis=-1, keepdims=True) + NORM_EPS) * g


def _norm_proj_kernel(x_ref, g_ref, w_ref, o_ref):
    h = _norm_rows(x_ref[...], g_ref[...]).astype(bf16)
    o_ref[...] = jnp.dot(h, w_ref[...], preferred_element_type=f32)


def norm_projection(x, g, w):
    m, d = x.shape
    n = w.shape[1]
    tm = 512
    assert m % tm == 0
    return pl.pallas_call(
        _norm_proj_kernel,
        out_shape=jax.ShapeDtypeStruct((m, n), f32),
        grid=(m // tm,),
        in_specs=[pl.BlockSpec((tm, d), lambda i: (i, 0)), pl.BlockSpec((1, d), lambda i: (0, 0)),
                  pl.BlockSpec((d, n), lambda i: (0, 0))],
        out_specs=pl.BlockSpec((tm, n), lambda i: (i, 0)),
        compiler_params=_cparams(("parallel",), 48),
        name="norm_proj",
    )(x, g.reshape(1, d).astype(f32), w)


def _proj_norm_kernel(a_ref, w_ref, res_ref, g_ref, x_ref, h_ref):
    x = res_ref[...] + jnp.dot(a_ref[...], w_ref[...], preferred_element_type=f32)
    x_ref[...] = x
    h_ref[...] = _norm_rows(x, g_ref[...]).astype(h_ref.dtype)


def projection_norm(a, w, res, g):
    m, kdim = a.shape
    n = w.shape[1]
    tm = 512
    assert m % tm == 0
    return pl.pallas_call(
        _proj_norm_kernel,
        out_shape=[jax.ShapeDtypeStruct((m, n), f32), jax.ShapeDtypeStruct((m, n), bf16)],
        grid=(m // tm,),
        in_specs=[pl.BlockSpec((tm, kdim), lambda i: (i, 0)), pl.BlockSpec((kdim, n), lambda i: (0, 0)),
                  pl.BlockSpec((tm, n), lambda i: (i, 0)), pl.BlockSpec((1, n), lambda i: (0, 0))],
        out_specs=[pl.BlockSpec((tm, n), lambda i: (i, 0)), pl.BlockSpec((tm, n), lambda i: (i, 0))],
        compiler_params=_cparams(("parallel",), 56),
        name="proj_norm",
    )(a, w, res, g.reshape(1, n).astype(f32))


def _prompt_attn_kernel(q_ref, kp_ref, kc_ref, vp_ref, vc_ref, o_ref, kk, vv, ob, lb, *, scale):
    c = pl.program_id(1)
    hg, blk, _ = q_ref.shape
    qb = ATTN_QB
    rows = lax.broadcasted_iota(jnp.int32, (qb, qb), 0)
    cols = lax.broadcasted_iota(jnp.int32, (qb, qb), 1)
    bias_prev = jnp.where(cols >= rows, 0.0, NEG_BIG)
    bias_cur = jnp.where(cols <= rows, 0.0, NEG_BIG)
    nt = (((1,), (1,)), ((), ()))
    scale2 = scale * LOG2E
    for h in range(hg):
        kk[0:blk, :] = kp_ref[h]
        kk[blk:2 * blk, :] = kc_ref[h]
        vv[0:blk, :] = vp_ref[h]
        vv[blk:2 * blk, :] = vc_ref[h]
        for g, (_, dil) in enumerate(DILATED_BRANCHES):

            def tile(t, carry, g=g, dil=dil):
                r = t % dil
                a = t // dil
                start = r + dil * qb * a
                q = q_ref[h, pl.ds(start, qb, stride=dil), :].astype(bf16)
                kstart = blk + start - dil * qb
                k = kk[pl.ds(kstart, 2 * qb, stride=dil), :].astype(bf16)
                v = vv[pl.ds(kstart, 2 * qb, stride=dil), :].astype(bf16)
                before_start = jnp.where((a > 0) | (c > 0), 0.0, NEG_BIG)
                bias = jnp.concatenate([bias_prev + before_start, bias_cur], axis=1)
                s = lax.dot_general(q, k, nt, preferred_element_type=f32) * scale2 + bias
                m = jnp.max(s, axis=-1, keepdims=True)
                p = jnp.exp2(s - m)
                l = jnp.sum(p, axis=-1, keepdims=True)
                o = jnp.dot(p.astype(bf16), v, preferred_element_type=f32) / l
                ob[g, pl.ds(start, qb, stride=dil), :] = o
                lb[g, pl.ds(start, qb, stride=dil), :] = jnp.broadcast_to((m + jnp.log2(l)) * LN2, (qb, LANES))
                return carry

            lax.fori_loop(0, blk // qb, tile, 0, unroll=True)
        step = 256
        for r0 in range(0, blk, step):
            ls = [lb[g, r0:r0 + step, :] for g in range(len(DILATED_BRANCHES))]
            mx = functools.reduce(jnp.maximum, ls)
            ws = [jnp.exp(x - mx) for x in ls]
            tot = functools.reduce(lambda x, y: x + y, ws)
            acc = ws[0] * ob[0, r0:r0 + step, :]
            for g in range(1, len(DILATED_BRANCHES)):
                acc = acc + ws[g] * ob[g, r0:r0 + step, :]
            o_ref[r0:r0 + step, h * HEAD_DIM:(h + 1) * HEAD_DIM] = acc / tot


def prompt_attention(q, k, v, seq):
    n_heads, m, _ = q.shape
    blk = MAX_DIL * ATTN_QB
    hg = 2
    assert seq % blk == 0 and n_heads % hg == 0 and m >= seq
    assert all(w == dil * ATTN_QB for w, dil in DILATED_BRANCHES)
    cur = pl.BlockSpec((hg, blk, HEAD_DIM), lambda i, c: (i, c, 0))
    prev = pl.BlockSpec((hg, blk, HEAD_DIM), lambda i, c: (i, jnp.maximum(c - 1, 0), 0))
    nbr = len(DILATED_BRANCHES)
    return pl.pallas_call(
        functools.partial(_prompt_attn_kernel, scale=1.0 / math.sqrt(HEAD_DIM)),
        out_shape=jax.ShapeDtypeStruct((seq, n_heads * HEAD_DIM), f32),
        grid=(n_heads // hg, seq // blk),
        in_specs=[cur, prev, cur, prev, cur],
        out_specs=pl.BlockSpec((blk, hg * HEAD_DIM), lambda i, c: (c, i)),
        scratch_shapes=[pltpu.VMEM((2 * blk, HEAD_DIM), f32), pltpu.VMEM((2 * blk, HEAD_DIM), f32),
                        pltpu.VMEM((nbr, blk, HEAD_DIM), f32), pltpu.VMEM((nbr, blk, LANES), f32)],
        compiler_params=_cparams(("parallel", "arbitrary"), 48),
        name="prompt_attn",
    )(q, k, k, v, v)


def _sample_key_positions(win_len, dec):
    half = MAX_DIL // 2
    ma, ra = np.meshgrid(np.arange(win_len // MAX_DIL), np.arange(half), indexing="ij")
    pos_a = (MAX_DIL * ma + ra).reshape(-1)
    mb0 = 3 * (win_len // MAX_DIL) // 4
    mb, rb = np.meshgrid(np.arange(mb0, win_len // MAX_DIL), np.arange(half, MAX_DIL), indexing="ij")
    pos_b = (MAX_DIL * mb + rb).reshape(-1)
    return np.concatenate([pos_a, pos_b, win_len + np.arange(dec)])


def _sample_bias(win_len, dec, n_heads):
    kpos = _sample_key_positions(win_len, dec)[:, None]
    qpos = PAST_LEN + (np.arange(LANES) % dec)[None, :]
    delta = qpos - kpos
    out = []
    for window, dil in DILATED_BRANCHES:
        ok = (delta >= 0) & (delta <= window) & (delta % dil == 0) & (kpos >= 0)
        out.append(np.where(ok, 0.0, NEG_BIG))
    out = np.stack(out).astype(np.float32)
    out[:, :, n_heads * dec:] = NEG_BIG
    return out


def _sample_attn_kernel(q_ref, kn_ref, vn_ref, ka_ref, kb_ref, va_ref, vb_ref, bias_ref, o_ref, *, scale):
    n_heads, dec, _ = q_ref.shape
    d = n_heads * HEAD_DIM
    nq = LANES

    def gather(ref):
        n = ref.shape[1] * ref.shape[2]
        return jnp.concatenate([ref[h].reshape(n, HEAD_DIM) for h in range(n_heads)], axis=-1).astype(bf16)

    def natural(ref):
        return jnp.concatenate([ref[h] for h in range(n_heads)], axis=-1)

    qt = jnp.concatenate([natural(q_ref)] * (nq // dec), axis=0)
    rowh = lax.broadcasted_iota(jnp.int32, (nq, d), 0) // dec
    colh = lax.broadcasted_iota(jnp.int32, (nq, d), 1) // HEAD_DIM
    qbd = jnp.where(rowh == colh, qt, 0.0).astype(bf16)
    nt = (((1,), (1,)), ((), ()))
    sa = lax.dot_general(gather(ka_ref), qbd, nt, preferred_element_type=f32)
    sb = lax.dot_general(gather(kb_ref), qbd, nt, preferred_element_type=f32)
    sn = lax.dot_general(natural(kn_ref).astype(bf16), qbd, nt, preferred_element_type=f32)
    na, nb = sa.shape[0], sb.shape[0]
    s = jnp.concatenate([sa, sb, sn], axis=0) * scale
    ps, ls, lses = [], [], []
    for g in range(len(DILATED_BRANCHES)):
        sg = s + bias_ref[g]
        m = jnp.max(sg, axis=0, keepdims=True)
        p = jnp.exp(sg - m)
        l = jnp.sum(p, axis=0, keepdims=True)
        ps.append(p)
        ls.append(l)
        lses.append(m + jnp.log(l))
    mx = functools.reduce(jnp.maximum, lses)
    ws = [jnp.exp(x - mx) for x in lses]
    tot = functools.reduce(lambda x, y: x + y, ws)
    pm = ps[0] * (ws[0] / (tot * ls[0]))
    for p, w, l in zip(ps[1:], ws[1:], ls[1:]):
        pm = pm + p * (w / (tot * l))
    pm = pm.astype(bf16)
    tn = (((0,), (0,)), ((), ()))
    o = lax.dot_general(pm[:na], gather(va_ref), tn, preferred_element_type=f32)
    o = o + lax.dot_general(pm[na:na + nb], gather(vb_ref), tn, preferred_element_type=f32)
    o = o + lax.dot_general(pm[na + nb:], natural(vn_ref).astype(bf16), tn, preferred_element_type=f32)
    for h in range(n_heads):
        hs = slice(h * HEAD_DIM, (h + 1) * HEAD_DIM)
        o_ref[:, hs] = o[h * dec:(h + 1) * dec, hs]


def sample_attention(q, k, v, cache_k, cache_v, row0):
    n_heads, m, _ = q.shape
    nbat, win_len = cache_k.shape[0], cache_k.shape[1]
    dec = (m - row0) // nbat
    assert win_len == PAST_LEN and dec == SUBLANES and row0 % dec == 0 and n_heads * dec <= LANES
    assert win_len % (4 * MAX_DIL) == 0 and DILATED_BRANCHES[1][0] <= win_len // 4
    half = MAX_DIL // 2
    mgrp = win_len // MAX_DIL
    d = n_heads * HEAD_DIM
    ck = cache_k.transpose(0, 2, 1, 3).reshape(nbat, n_heads, mgrp, MAX_DIL, HEAD_DIM)
    cv = cache_v.transpose(0, 2, 1, 3).reshape(nbat, n_heads, mgrp, MAX_DIL, HEAD_DIM)
    bias = jnp.asarray(_sample_bias(win_len, dec, n_heads))
    new = pl.BlockSpec((n_heads, dec, HEAD_DIM), lambda b: (0, row0 // dec + b, 0))
    part_a = pl.BlockSpec((None, n_heads, mgrp, half, HEAD_DIM), lambda b: (b, 0, 0, 0, 0))
    part_b = pl.BlockSpec((None, n_heads, mgrp // 4, half, HEAD_DIM), lambda b: (b, 0, 3, 1, 0))
    out = pl.pallas_call(
        functools.partial(_sample_attn_kernel, scale=1.0 / math.sqrt(HEAD_DIM)),
        out_shape=jax.ShapeDtypeStruct((nbat, dec, d), f32),
        grid=(nbat,),
        in_specs=[new, new, new, part_a, part_b, part_a, part_b, pl.BlockSpec(bias.shape, lambda b: (0, 0, 0))],
        out_specs=pl.BlockSpec((None, dec, d), lambda b: (b, 0, 0)),
        compiler_params=_cparams(("parallel",), 56),
        name="sample_attn",
    )(q, k, v, ck, ck, cv, cv, bias)
    return out.reshape(nbat * dec, d)


def _s5_discretise(a_re, a_im, dt):
    mag = jnp.exp(a_re * dt)
    lam_re = mag * jnp.cos(a_im * dt)
    lam_im = mag * jnp.sin(a_im * dt)
    num_re = lam_re - 1.0
    num_im = lam_im
    den = a_re * a_re + a_im * a_im
    f_re = (num_re * a_re + num_im * a_im) / den
    f_im = (num_im * a_re - num_re * a_im) / den
    return lam_re, lam_im, f_re, f_im


def _cmul(ar, ai, br, bi):
    return ar * br - ai * bi, ar * bi + ai * br


def _s5_operators_kernel(arc, aic, ldc, btr, bti, ccr, cci, arb, aib, ldb, cbr, cbi, ars, ais, lds,
                         w_ref, p_ref, q_ref, lr_ref, li_ref):
    t_len = SSM_T
    nt = (((1,), (1,)), ((), ()))
    hi = lax.Precision.HIGHEST
    gsz, ssz = SSM_GROUP, SSM_STATE

    def group_mask(shape, row_div, col_div):
        r = lax.broadcasted_iota(jnp.int32, shape, 0) // row_div
        c = lax.broadcasted_iota(jnp.int32, shape, 1) // col_div
        return r == c

    lr, li, fr, fi = _s5_discretise(arc[...], aic[...], jnp.exp(ldc[...]))
    er, ei = _cmul(fr, fi, btr[...], bti[...])
    c_re, c_im = ccr[...], cci[...]
    mask_w = group_mask((LANES, LANES), gsz, gsz)
    mask_p = group_mask((LANES, STATE_BLOCK), gsz, ssz)
    zero_blk = jnp.zeros((LANES, LANES), bf16)
    d_blocks, e_pows = [], []
    for n in range(t_len):
        e_pows.append((er, ei))
        dn = (lax.dot_general(er, c_re, nt, precision=hi, preferred_element_type=f32)
              - lax.dot_general(ei, c_im, nt, precision=hi, preferred_element_type=f32))
        d_blocks.append(jnp.where(mask_w, dn, 0.0).astype(bf16))
        er, ei = _cmul(lr, li, er, ei)
    for ti in range(t_len):
        for to in range(t_len):
            w_ref[ti * LANES:(ti + 1) * LANES, to * LANES:(to + 1) * LANES] = (
                d_blocks[to - ti] if to >= ti else zero_blk)
        pr, pi = e_pows[t_len - 1 - ti]
        p_ref[ti * LANES:(ti + 1) * LANES, 0:STATE_BLOCK] = jnp.where(
            mask_p, jnp.concatenate([pr] * GROUPS_PER_BLOCK, axis=1), 0.0).astype(bf16)
        p_ref[ti * LANES:(ti + 1) * LANES, STATE_BLOCK:] = jnp.where(
            mask_p, jnp.concatenate([pi] * GROUPS_PER_BLOCK, axis=1), 0.0).astype(bf16)
    lbr, lbi, _, _ = _s5_discretise(arb[...], aib[...], jnp.exp(ldb[...]))
    mask_q = group_mask((STATE_BLOCK, LANES), ssz, gsz)
    qr, qi = cbr[...], cbi[...]
    for t in range(t_len):
        qr, qi = _cmul(qr, qi, lbr, lbi)
        q_ref[0:STATE_BLOCK, t * LANES:(t + 1) * LANES] = jnp.where(
            mask_q, jnp.concatenate([qr] * GROUPS_PER_BLOCK, axis=0), 0.0).astype(bf16)
        q_ref[STATE_BLOCK:, t * LANES:(t + 1) * LANES] = jnp.where(
            mask_q, jnp.concatenate([-qi] * GROUPS_PER_BLOCK, axis=0), 0.0).astype(bf16)
    lsr, lsi, _, _ = _s5_discretise(ars[...], ais[...], jnp.exp(lds[...]))
    pr, pi = lsr, lsi
    for _ in range(t_len - 1):
        pr, pi = _cmul(pr, pi, lsr, lsi)
    lr_ref[...] = pr
    li_ref[...] = pi


def s5_chunk_operators(a_re, a_im, log_dt, b_re, b_im, c_re, c_im):
    g, p = a_re.shape
    ch = b_re.shape[2]
    j = GROUPS_PER_BLOCK
    nb = g // j
    assert p == SSM_STATE and ch == SSM_GROUP and g % j == 0
    t = SSM_T
    a_re, a_im, log_dt, b_re, b_im, c_re, c_im = (x.astype(f32) for x in (a_re, a_im, log_dt, b_re, b_im, c_re, c_im))
    by_row = lambda x: jnp.repeat(x.reshape(nb, j, p), ch, axis=1)
    by_col = lambda x: jnp.repeat(x.reshape(nb, j, p).transpose(0, 2, 1), ch, axis=2)
    ld = log_dt.reshape(nb, j)
    ld_row = jnp.repeat(ld, ch, axis=1)[:, :, None]
    ld_col = jnp.repeat(ld, ch, axis=1)[:, None, :]
    ld_flat = jnp.repeat(ld, p, axis=1)[:, None, :]
    bt = lambda x: x.reshape(nb, j, p, ch).transpose(0, 1, 3, 2).reshape(nb, j * ch, p)
    c_row = lambda x: x.reshape(nb, j * ch, p)
    c_col = lambda x: x.reshape(nb, j, ch, p).transpose(0, 3, 1, 2).reshape(nb, p, j * ch)
    flat = lambda x: x.reshape(nb, 1, j * p)
    ops = [by_row(a_re), by_row(a_im), ld_row, bt(b_re), bt(b_im), c_row(c_re), c_row(c_im),
           by_col(a_re), by_col(a_im), ld_col, c_col(c_re), c_col(c_im),
           flat(a_re), flat(a_im), ld_flat]
    blk = lambda arr: pl.BlockSpec((None,) + arr.shape[1:], lambda i: (i, 0, 0))
    shp = jax.ShapeDtypeStruct
    sb = STATE_BLOCK
    outs = [shp((nb, t * LANES, t * LANES), bf16), shp((nb, t * LANES, 2 * sb), bf16),
            shp((nb, 2 * sb, t * LANES), bf16), shp((nb, 1, sb), f32), shp((nb, 1, sb), f32)]
    return pl.pallas_call(
        _s5_operators_kernel,
        out_shape=outs,
        grid=(nb,),
        in_specs=[blk(x) for x in ops],
        out_specs=[blk(x) for x in outs],
        compiler_params=_cparams(("parallel",), 32),
        name="s5_operators",
    )(*ops)


def _s5_kernel(u_ref, w_ref, p_ref, q_ref, lr_ref, li_ref, dsk_ref, h0r_ref, h0i_ref,
               gl_ref, fpr_ref, fpi_ref, fsr_ref, fsi_ref, h_scr, *, segs_per_dot):
    seg, pitch = S5_SEG, S5_SEG_PITCH
    n_seg = u_ref.shape[0] // (SSM_T * seg)
    n_pseg = n_seg - 1
    nsl = 2 * STATE_BLOCK // LANES
    half = nsl // 2
    lr, li = lr_ref[...], li_ref[...]
    slab = lambda x, l: x[:, l * LANES:(l + 1) * LANES]

    def ucat(s0):
        r0, n = s0 * seg, segs_per_dot * seg
        return jnp.concatenate([u_ref[pl.ds(SSM_T * r0 + t, n, stride=SSM_T), :] for t in range(SSM_T)], axis=-1)

    def seg_rows(s):
        return slice(s * pitch, s * pitch + seg)

    for s0 in range(0, n_seg, segs_per_dot):
        res = jnp.dot(ucat(s0).astype(bf16), p_ref[...], preferred_element_type=f32)
        for s in range(segs_per_dot):
            for l in range(nsl):
                h_scr[l, seg_rows(s0 + s), :] = slab(res[s * seg:(s + 1) * seg], l)

    lam_r = [jnp.broadcast_to(slab(lr, l), (SUBLANES, LANES)) for l in range(half)]
    lam_i = [jnp.broadcast_to(slab(li, l), (SUBLANES, LANES)) for l in range(half)]

    def load(k):
        return [h_scr[l, pl.ds(k, SUBLANES, stride=pitch), :] for l in range(nsl)]

    def advance(c, x, mr, mi):
        re = [mr[l] * c[l] - mi[l] * c[l + half] + x[l] for l in range(half)]
        im = [mr[l] * c[l + half] + mi[l] * c[l] + x[l + half] for l in range(half)]
        return tuple(re + im)

    zeros = tuple(jnp.zeros((SUBLANES, LANES), f32) for _ in range(nsl))
    ends = lax.fori_loop(0, seg, lambda k, c: advance(c, load(k), lam_r, lam_i), zeros, unroll=8)

    pr, pi = [slab(lr, l) for l in range(half)], [slab(li, l) for l in range(half)]
    for _ in range(seg.bit_length() - 1):
        sq = [_cmul(a, b, a, b) for a, b in zip(pr, pi)]
        pr, pi = [x[0] for x in sq], [x[1] for x in sq]
    cur = tuple(jnp.zeros((1, LANES), f32) for _ in range(nsl))
    starts = [cur]
    for j in range(n_pseg):
        cur = advance(cur, [e[j:j + 1, :] for e in ends], pr, pi)
        starts.append(cur)
    fpr_ref[...] = jnp.concatenate(starts[n_pseg][:half], axis=-1)
    fpi_ref[...] = jnp.concatenate(starts[n_pseg][half:], axis=-1)
    init = tuple(jnp.concatenate([starts[j][l] for j in range(n_pseg)], axis=0) for l in range(nsl))

    def sweep(k, c):
        x = load(k)
        for l in range(nsl):
            h_scr[l, pl.ds(k, SUBLANES, stride=pitch), :] = c[l]
        return advance(c, x, lam_r, lam_i)

    lax.fori_loop(0, seg, sweep, init, unroll=8)

    h0r, h0i = h0r_ref[...], h0i_ref[...]
    last = seg_rows(n_pseg)
    inr = jnp.concatenate([h_scr[l, last, :] for l in range(half)], axis=-1)
    ini = jnp.concatenate([h_scr[l + half, last, :] for l in range(half)], axis=-1)
    fsr_ref[...] = lr * h0r - li * h0i + inr
    fsi_ref[...] = lr * h0i + li * h0r + ini
    for l in range(half):
        h_scr[l, last, :] = slab(h0r, l)
        h_scr[l + half, last, :] = slab(h0i, l)

    dsk = dsk_ref[...]
    for s0 in range(0, n_seg, segs_per_dot):
        u = ucat(s0)
        hp = jnp.concatenate([jnp.concatenate([h_scr[l, seg_rows(s0 + s), :] for l in range(nsl)], axis=-1)
                              for s in range(segs_per_dot)], axis=0)
        y = jnp.dot(u.astype(bf16), w_ref[...], preferred_element_type=f32)
        y = y + jnp.dot(hp.astype(bf16), q_ref[...], preferred_element_type=f32)
        gl = jax.nn.gelu(y + dsk * u)
        r0, n = s0 * seg, segs_per_dot * seg
        for t in range(SSM_T):
            gl_ref[pl.ds(SSM_T * r0 + t, n, stride=SSM_T), :] = gl[:, t * LANES:(t + 1) * LANES]


def s5_mixer(u, ops, d_skip, h0_re, h0_im, n_prompt):
    w, pm, qm, lr, li = ops
    m, dssm = u.shape
    nb = dssm // LANES
    t = SSM_T
    nbat = h0_re.shape[0]
    n_seg = m // (t * S5_SEG)
    segs_per_dot = 3
    assert n_prompt == SUBLANES * S5_SEG * t and nbat == S5_SEG and m == n_prompt + nbat * t
    assert n_seg % segs_per_dot == 0
    sb = STATE_BLOCK
    dsk = jnp.tile(d_skip.astype(f32).reshape(nb, 1, LANES), (1, 1, t))
    h0r = h0_re.astype(f32).reshape(nbat, nb * sb)
    h0i = h0_im.astype(f32).reshape(nbat, nb * sb)
    blk = lambda shape: pl.BlockSpec((None,) + shape, lambda i: (i, 0, 0))
    ublk = pl.BlockSpec((m, LANES), lambda i: (0, i))
    hblk = pl.BlockSpec((nbat, sb), lambda i: (0, i))
    fblk = pl.BlockSpec((1, sb), lambda i: (0, i))
    shp = jax.ShapeDtypeStruct
    return pl.pallas_call(
        functools.partial(_s5_kernel, segs_per_dot=segs_per_dot),
        out_shape=[shp((m, dssm), f32), shp((1, nb * sb), f32), shp((1, nb * sb), f32),
                   shp((nbat, nb * sb), f32), shp((nbat, nb * sb), f32)],
        grid=(nb,),
        in_specs=[ublk, blk((t * LANES, t * LANES)), blk((t * LANES, 2 * sb)), blk((2 * sb, t * LANES)),
                  blk((1, sb)), blk((1, sb)), blk((1, t * LANES)), hblk, hblk],
        out_specs=[ublk, fblk, fblk, hblk, hblk],
        scratch_shapes=[pltpu.VMEM((2 * sb // LANES, n_seg * S5_SEG_PITCH, LANES), f32)],
        compiler_params=_cparams(("parallel",), 56),
        name="s5_mixer",
    )(u, w, pm, qm, lr, li, dsk, h0r, h0i)


def _cross_attn_cache_kernel(q_ref, k_ref, v_ref, o_ref, *, scale):
    gb, n, nh, hd = k_ref.shape
    dec = q_ref.shape[1]
    rowh = lax.broadcasted_iota(jnp.int32, (n * nh, nh * dec), 0) % nh
    colh = lax.broadcasted_iota(jnp.int32, (n * nh, nh * dec), 1) // dec
    same = rowh == colh
    for g in range(gb):
        kf = k_ref[g].reshape(n * nh, hd).astype(bf16)
        vf = v_ref[g].reshape(n * nh, hd).astype(bf16)
        q = q_ref[g]
        qs = jnp.concatenate([q[:, h * hd:(h + 1) * hd] for h in range(nh)], axis=0).astype(bf16)
        s = lax.dot_general(kf, qs, (((1,), (1,)), ((), ())), preferred_element_type=f32) * scale
        s = jnp.where(same, s, NEG_BIG)
        m = jnp.max(s, axis=0, keepdims=True)
        p = jnp.exp(s - m)
        p = p / jnp.sum(p, axis=0, keepdims=True)
        o = lax.dot_general(p.astype(bf16), vf, (((0,), (0,)), ((), ())), preferred_element_type=f32)
        for h in range(nh):
            o_ref[g, :, h * hd:(h + 1) * hd] = o[h * dec:(h + 1) * dec, :].astype(o_ref.dtype)


def _cross_attn_kernel(q_ref, k_ref, v_ref, o_ref, *, scale):
    q = q_ref[...].astype(bf16)
    for h in range(MEM_HEADS):
        hs = slice(h * MEM_HEAD_DIM, (h + 1) * MEM_HEAD_DIM)
        k = k_ref[:, :, hs].astype(bf16)
        v = v_ref[:, :, hs].astype(bf16)
        s = jnp.einsum("gqd,gkd->gqk", q[:, :, hs], k, preferred_element_type=f32) * scale
        m = jnp.max(s, axis=-1, keepdims=True)
        p = jnp.exp(s - m)
        p = p / jnp.sum(p, axis=-1, keepdims=True)
        o = jnp.einsum("gqk,gkd->gqd", p.astype(bf16), v, preferred_element_type=f32)
        o_ref[:, :, hs] = o.astype(o_ref.dtype)


def cross_attention(q3, k, v, nq, gb, tq, q_block0, name):
    g = k.shape[0]
    e = q3.shape[2]
    kblock = (gb,) + k.shape[1:]
    kmap = (lambda i, j: (i, 0, 0)) if k.ndim == 3 else (lambda i, j: (i, 0, 0, 0))
    body = _cross_attn_kernel if k.ndim == 3 else _cross_attn_cache_kernel
    return pl.pallas_call(
        functools.partial(body, scale=1.0 / math.sqrt(MEM_HEAD_DIM)),
        out_shape=jax.ShapeDtypeStruct((g, nq, e), bf16),
        grid=(g // gb, nq // tq),
        in_specs=[pl.BlockSpec((gb, tq, e), lambda i, j: (q_block0 + i, j, 0)),
                  pl.BlockSpec(kblock, kmap), pl.BlockSpec(kblock, kmap)],
        out_specs=pl.BlockSpec((gb, tq, e), lambda i, j: (i, j, 0)),
        compiler_params=_cparams(("parallel", "parallel"), 48),
        name=name,
    )(q3, k, v)


def _rotary_tables(pos):
    inv = 1.0 / (ROPE_THETA ** (jnp.arange(0, HEAD_DIM, 2, dtype=f32) / HEAD_DIM))
    ang = pos.astype(f32)[:, None] * inv[None, :]
    cos, sin = jnp.cos(ang), jnp.sin(ang)
    return jnp.concatenate([cos, cos], axis=-1), jnp.concatenate([-sin, sin], axis=-1)


def kernel(x_prompt, x_sample, cache_win_k, cache_win_v, state_ssm_re, state_ssm_im, cache_mem_k, cache_mem_v,
           mem_prompt, g_mix, w_in, a_re, a_im, log_dt, b_re, b_im, c_re, c_im, d_skip, w_glu, b_glu,
           g_attn_out, g_ssm_out, w_out, g_cross, g_mem, w_mq, w_mk, w_mv, w_mo, g_ffn, w_up, w_down, g_final):
    bp, sp, dm = x_prompt.shape
    nbat, dec, _ = x_sample.shape
    assert bp == 1
    ns = nbat * dec
    n_groups = a_re.shape[0]
    d_ssm = n_groups * SSM_GROUP
    d_attn = dm - d_ssm
    n_heads = d_attn // HEAD_DIM
    e_mem = MEM_HEADS * MEM_HEAD_DIM
    n_mem = mem_prompt.shape[1]
    keep = min(PAST_LEN, sp)

    xp, xs = x_prompt.reshape(sp, dm), x_sample.reshape(ns, dm)
    mtot = sp + ns
    pos = jnp.concatenate([jnp.arange(sp, dtype=jnp.int32), PAST_LEN + jnp.tile(jnp.arange(dec, dtype=jnp.int32), nbat)])
    cos2, sin2 = _rotary_tables(pos)
    wb = lambda w_: w_.astype(bf16)

    h = rmsnorm2(xp, xs, g_mix, bf16)
    rot =((cos2, "rowtab"), (sin2, "rowtab"))
    q = matmul(h, w_in, _ep_rotary, rot, head_major=True, cols=(0, d_attn), tn_cap=512, name="proj_q")
    k = matmul(h, w_in, _ep_rotary, rot, head_major=True, cols=(d_attn, d_attn), tn_cap=512, name="proj_k")
    v = matmul(h, w_in, head_major=True, cols=(2 * d_attn, d_attn), tn_cap=512, name="proj_v")
    u = matmul(h, w_in, cols=(3 * d_attn, d_ssm), tn_cap=512, name="proj_u")

    attn_p = prompt_attention(q, k, v, sp)
    attn_s = sample_attention(q, k, v, cache_win_k, cache_win_v, sp)
    na = rmsnorm2(attn_p, attn_s, g_attn_out, bf16)

    ops = s5_chunk_operators(a_re, a_im, log_dt, b_re, b_im, c_re, c_im)
    gl, fpr, fpi, fsr, fsi = s5_mixer(u, ops, d_skip, state_ssm_re, state_ssm_im, sp)
    y_ssm = matmul(gl, wb(w_glu), _ep_glu, ((gl, "tile"), (b_glu.astype(f32).reshape(1, d_ssm), "col")), tn_cap=512,
                   name="glu")
    ny = rmsnorm(y_ssm, g_ssm_out, bf16)

    x1 = out_projection(na, ny, wb(w_out[:d_attn]), wb(w_out[d_attn:]), xp, xs)

    mem_n = rmsnorm(mem_prompt.reshape(n_mem, dm), g_mem, bf16)
    mem_k_p = matmul(mem_n, wb(w_mk), name="mem_k")
    mem_v_p = matmul(mem_n, wb(w_mv), name="mem_v")
    qm = norm_projection(x1, g_cross, wb(w_mq))
    o_p = cross_attention(qm.reshape(1, mtot, e_mem), mem_k_p.reshape(1, n_mem, e_mem), mem_v_p.reshape(1, n_mem, e_mem),
                          sp, 1, 512, 0, "cross_attn_prompt")
    o_s = cross_attention(qm.reshape(mtot // dec, dec, e_mem), cache_mem_k, cache_mem_v,
                          dec, 8, dec, sp // dec // 8, "cross_attn_sample")
    o_c = jnp.concatenate([o_p.reshape(sp, e_mem), o_s.reshape(ns, e_mem)], axis=0)
    x2, hf = projection_norm(o_c, wb(w_mo), x1, g_ffn)

    hid = matmul(hf, w_up, _ep_relu2, out_dtype=bf16, tm_pref=1536, tn_cap=512, name="ffn_up")
    x3 = matmul(hid, wb(w_down), _ep_residual, ((x2, "tile"),), tm_pref=1536, name="ffn_down")
    y_p = rmsnorm(x3, g_final, f32, 0, sp)
    y_s = rmsnorm(x3, g_final, f32, sp, ns)

    hd = (n_heads, HEAD_DIM)
    natural = lambda t, r0, r1: t[:, r0:r1].transpose(1, 0, 2)
    return (y_p.reshape(1, sp, dm), y_s.reshape(nbat, dec, dm),
            natural(k, sp - keep, sp).reshape(1, keep, *hd), natural(v, sp - keep, sp).reshape(1, keep, *hd),
            fpr.reshape(1, n_groups, SSM_STATE), fpi.reshape(1, n_groups, SSM_STATE),
            mem_k_p.reshape(1, n_mem, MEM_HEADS, MEM_HEAD_DIM), mem_v_p.reshape(1, n_mem, MEM_HEADS, MEM_HEAD_DIM),
            natural(k, sp, mtot).reshape(nbat, dec, *hd), natural(v, sp, mtot).reshape(nbat, dec, *hd),
            fsr.reshape(nbat, n_groups, SSM_STATE), fsi.reshape(nbat, n_groups, SSM_STATE))
```

```python
import functools
import math

import numpy as np
import jax
import jax.numpy as jnp
from jax import lax
from jax.experimental import pallas as pl
from jax.experimental.pallas import tpu as pltpu

f32 = jnp.float32
bf16 = jnp.bfloat16

HEAD_DIM = 128
SSM_GROUP = 16
SSM_STATE = 64
DILATED_BRANCHES = ((128, 1), (512, 4), (2048, 16))
MAX_DIL = 16
PAST_LEN = 2048
MEM_HEADS = 4
MEM_HEAD_DIM = 128
ROPE_THETA = 10000.0
NORM_EPS = 1e-6
NEG_BIG = -1e30
LOG2E = math.log2(math.e)
LN2 = math.log(2.0)

LANES = 128
SUBLANES = 8
SSM_T = 8
GROUPS_PER_BLOCK = LANES // SSM_GROUP
STATE_BLOCK = GROUPS_PER_BLOCK * SSM_STATE
ATTN_QB = 128
S5_SEG = 128
S5_SEG_PITCH = S5_SEG + SUBLANES
V7X_VMEM_BYTES = 64 * 1024 * 1024


def _cparams(sem, vmem_mb):
    assert vmem_mb * 1024 * 1024 < V7X_VMEM_BYTES
    return pltpu.CompilerParams(dimension_semantics=sem, vmem_limit_bytes=vmem_mb * 1024 * 1024)


def _rmsnorm_kernel(x_ref, g_ref, o_ref):
    x = x_ref[...].astype(f32)
    y = x * lax.rsqrt(jnp.mean(x * x, axis=-1, keepdims=True) + NORM_EPS)
    o_ref[...] = (y * g_ref[...]).astype(o_ref.dtype)


def rmsnorm(x, g, out_dtype, row_start=0, rows=None, tm=256):
    m, d = x.shape
    rows = m if rows is None else rows
    tm = min(tm, rows)
    assert rows % tm == 0 and row_start % tm == 0
    off = row_start // tm
    return pl.pallas_call(
        _rmsnorm_kernel,
        out_shape=jax.ShapeDtypeStruct((rows, d), out_dtype),
        grid=(rows // tm,),
        in_specs=[pl.BlockSpec((tm, d), lambda i: (i + off, 0)),
                  pl.BlockSpec((1, d), lambda i: (0, 0))],
        out_specs=pl.BlockSpec((tm, d), lambda i: (i, 0)),
        compiler_params=_cparams(("parallel",), 40),
        name="rmsnorm",
    )(x, g.reshape(1, d).astype(f32))


def _rmsnorm2_kernel(xa_ref, xb_ref, g_ref, o_ref, *, na):
    i = pl.program_id(0)

    @pl.when(i < na)
    def _():
        _rmsnorm_kernel(xa_ref, g_ref, o_ref)

    @pl.when(i >= na)
    def _():
        _rmsnorm_kernel(xb_ref, g_ref, o_ref)


def rmsnorm2(xa, xb, g, out_dtype, tm=256):
    (ma, d), (mb, _) = xa.shape, xb.shape
    assert ma % tm == 0 and mb % tm == 0
    na = ma // tm
    return pl.pallas_call(
        functools.partial(_rmsnorm2_kernel, na=na),
        out_shape=jax.ShapeDtypeStruct((ma + mb, d), out_dtype),
        grid=((ma + mb) // tm,),
        in_specs=[pl.BlockSpec((tm, d), lambda i: (jnp.minimum(i, na - 1), 0)),
                  pl.BlockSpec((tm, d), lambda i: (jnp.maximum(i - na, 0), 0)),
                  pl.BlockSpec((1, d), lambda i: (0, 0))],
        out_specs=pl.BlockSpec((tm, d), lambda i: (i, 0)),
        compiler_params=_cparams(("arbitrary",), 40),
        name="rmsnorm2",
    )(xa, xb, g.reshape(1, d).astype(f32))


def _ep_none(acc):
    return acc


def _ep_residual(acc, res):
    return acc + res


def _ep_relu2(acc):
    r = jnp.maximum(acc, 0.0)
    return r * r


def _ep_glu(acc, gl, bias):
    return gl * jax.nn.sigmoid(acc + bias)


def _ep_rotary(acc, cos2, sin2):
    parts = []
    for h in range(acc.shape[1] // HEAD_DIM):
        blk = acc[:, h * HEAD_DIM:(h + 1) * HEAD_DIM]
        parts.append(blk * cos2 + pltpu.roll(blk, HEAD_DIM // 2, 1) * sin2)
    return jnp.concatenate(parts, axis=-1)


def _mm_kernel(*refs, nk, epilogue, n_extra, head_major, cast_a_once):
    a_ref, b_ref = refs[0], refs[1]
    extra = refs[2:2 + n_extra]
    o_ref = refs[2 + n_extra]
    if cast_a_once:
        a_bf = refs[3 + n_extra]

        @pl.when(pl.program_id(1) == 0)
        def _():
            a_bf[...] = a_ref[...].astype(bf16)

        a_ref = a_bf

    def prod():
        return jnp.dot(a_ref[...].astype(bf16), b_ref[...].astype(bf16), preferred_element_type=f32)

    def finish(acc):
        res = epilogue(acc, *[e[...] for e in extra]).astype(o_ref.dtype)
        if head_major:
            for h in range(o_ref.shape[0]):
                o_ref[h] = res[:, h * LANES:(h + 1) * LANES]
        else:
            o_ref[...] = res

    if nk == 1:
        finish(prod())
        return
    k = pl.program_id(2)

    @pl.when(k == 0)
    def _():
        o_ref[...] = prod()

    @pl.when((k > 0) & (k < nk - 1))
    def _():
        o_ref[...] += prod()

    @pl.when(k == nk - 1)
    def _():
        finish(o_ref[...] + prod())


def _largest_tile(n, cap):
    best = None
    for t in range(LANES, cap + 1, LANES):
        if n % t == 0:
            best = t
    assert best is not None, (n, cap)
    return best


def matmul(a, b, epilogue=_ep_none, extras=(), out_dtype=f32, head_major=False, cols=None, tm_pref=1024,
           tn_cap=1024, name="matmul"):
    m, kdim = a.shape
    col0, n = (0, b.shape[1]) if cols is None else cols
    tm = next((t for t in (tm_pref, 1024) if m % t == 0), m)
    tn = _largest_tile(math.gcd(n, col0) if col0 else n, tn_cap)
    tk = kdim if kdim <= 4096 else 2048
    assert m % tm == 0 and kdim % tk == 0 and col0 % tn == 0 and n % tn == 0
    nk = kdim // tk
    assert nk == 1 or (out_dtype == f32 and not head_major)
    joff = col0 // tn
    in_specs = [pl.BlockSpec((tm, tk), lambda i, j, k: (i, k)),
                pl.BlockSpec((tk, tn), lambda i, j, k: (k, j + joff))]
    ops = [a, b]
    for arr, kind in extras:
        if kind == "tile":
            in_specs.append(pl.BlockSpec((tm, tn), lambda i, j, k: (i, j)))
        elif kind == "col":
            in_specs.append(pl.BlockSpec((1, tn), lambda i, j, k: (0, j)))
        else:
            in_specs.append(pl.BlockSpec((tm, arr.shape[1]), lambda i, j, k: (i, 0)))
        ops.append(arr)
    if head_major:
        out_shape = jax.ShapeDtypeStruct((n // LANES, m, LANES), out_dtype)
        out_spec = pl.BlockSpec((tn // LANES, tm, LANES), lambda i, j, k: (j, i, 0))
    else:
        out_shape = jax.ShapeDtypeStruct((m, n), out_dtype)
        out_spec = pl.BlockSpec((tm, tn), lambda i, j, k: (i, j))
    cast_a_once = a.dtype == f32 and nk == 1 and n // tn > 1
    return pl.pallas_call(
        functools.partial(_mm_kernel, nk=nk, epilogue=epilogue, n_extra=len(extras), head_major=head_major,
                          cast_a_once=cast_a_once),
        out_shape=out_shape,
        grid=(m // tm, n // tn, nk),
        in_specs=in_specs,
        out_specs=out_spec,
        scratch_shapes=[pltpu.VMEM((tm, tk), bf16)] if cast_a_once else [],
        compiler_params=_cparams(("parallel", "arbitrary", "arbitrary"), 60),
        name=name,
    )(*ops)


def _out_proj_kernel(na_ref, ny_ref, wa_ref, wy_ref, xa_ref, xb_ref, o_ref, *, nblk_a):
    i = pl.program_id(0)

    def with_residual(x_ref):
        acc = jnp.dot(na_ref[...], wa_ref[...], preferred_element_type=f32)
        acc = acc + jnp.dot(ny_ref[...], wy_ref[...], preferred_element_type=f32)
        o_ref[...] = acc + x_ref[...]

    @pl.when(i < nblk_a)
    def _():
        with_residual(xa_ref)

    @pl.when(i >= nblk_a)
    def _():
        with_residual(xb_ref)


def out_projection(na, ny, w_a, w_y, xa, xb):
    m, ka = na.shape
    ky = ny.shape[1]
    n = w_a.shape[1]
    tm = 1024
    tn = _largest_tile(n, 1024)
    assert m % tm == 0 and xa.shape[0] % tm == 0 and xa.shape[0] + xb.shape[0] == m
    nblk_a = xa.shape[0] // tm
    last_j = n // tn - 1
    xa_spec = pl.BlockSpec((tm, tn), lambda i, j: (jnp.minimum(i, nblk_a - 1), jnp.where(i < nblk_a, j, last_j)))
    xb_spec = pl.BlockSpec((tm, tn), lambda i, j: (jnp.maximum(i - nblk_a, 0), jnp.where(i < nblk_a, 0, j)))
    return pl.pallas_call(
        functools.partial(_out_proj_kernel, nblk_a=nblk_a),
        out_shape=jax.ShapeDtypeStruct((m, n), f32),
        grid=(m // tm, n // tn),
        in_specs=[pl.BlockSpec((tm, ka), lambda i, j: (i, 0)), pl.BlockSpec((tm, ky), lambda i, j: (i, 0)),
                  pl.BlockSpec((ka, tn), lambda i, j: (0, j)), pl.BlockSpec((ky, tn), lambda i, j: (0, j)),
                  xa_spec, xb_spec],
        out_specs=pl.BlockSpec((tm, tn), lambda i, j: (i, j)),
        compiler_params=_cparams(("arbitrary", "arbitrary"), 60),
        name="out_proj",
    )(na, ny, w_a, w_y, xa, xb)


def _norm_rows(x, g):
    return x * lax.rsqrt(jnp.mean(x * x, axis=-1, keepdims=True) + NORM_EPS) * g


def _norm_proj_kernel(x_ref, g_ref, w_ref, o_ref):
    h = _norm_rows(x_ref[...], g_ref[...]).astype(bf16)
    o_ref[...] = jnp.dot(h, w_ref[...], preferred_element_type=f32)


def norm_projection(x, g, w):
    m, d = x.shape
    n = w.shape[1]
    tm = 512
    assert m % tm == 0
    return pl.pallas_call(
        _norm_proj_kernel,
        out_shape=jax.ShapeDtypeStruct((m, n), f32),
        grid=(m // tm,),
        in_specs=[pl.BlockSpec((tm, d), lambda i: (i, 0)), pl.BlockSpec((1, d), lambda i: (0, 0)),
                  pl.BlockSpec((d, n), lambda i: (0, 0))],
        out_specs=pl.BlockSpec((tm, n), lambda i: (i, 0)),
        compiler_params=_cparams(("parallel",), 48),
        name="norm_proj",
    )(x, g.reshape(1, d).astype(f32), w)


def _proj_norm_kernel(a_ref, w_ref, res_ref, g_ref, x_ref, h_ref):
    x = res_ref[...] + jnp.dot(a_ref[...], w_ref[...], preferred_element_type=f32)
    x_ref[...] = x
    h_ref[...] = _norm_rows(x, g_ref[...]).astype(h_ref.dtype)


def projection_norm(a, w, res, g):
    m, kdim = a.shape
    n = w.shape[1]
    tm = 512
    assert m % tm == 0
    return pl.pallas_call(
        _proj_norm_kernel,
        out_shape=[jax.ShapeDtypeStruct((m, n), f32), jax.ShapeDtypeStruct((m, n), bf16)],
        grid=(m // tm,),
        in_specs=[pl.BlockSpec((tm, kdim), lambda i: (i, 0)), pl.BlockSpec((kdim, n), lambda i: (0, 0)),
                  pl.BlockSpec((tm, n), lambda i: (i, 0)), pl.BlockSpec((1, n), lambda i: (0, 0))],
        out_specs=[pl.BlockSpec((tm, n), lambda i: (i, 0)), pl.BlockSpec((tm, n), lambda i: (i, 0))],
        compiler_params=_cparams(("parallel",), 56),
        name="proj_norm",
    )(a, w, res, g.reshape(1, n).astype(f32))


def _prompt_attn_kernel(q_ref, kp_ref, kc_ref, vp_ref, vc_ref, o_ref, kk, vv, ob, lb, *, scale):
    c = pl.program_id(1)
    hg, blk, _ = q_ref.shape
    qb = ATTN_QB
    rows = lax.broadcasted_iota(jnp.int32, (qb, qb), 0)
    cols = lax.broadcasted_iota(jnp.int32, (qb, qb), 1)
    bias_prev = jnp.where(cols >= rows, 0.0, NEG_BIG)
    bias_cur = jnp.where(cols <= rows, 0.0, NEG_BIG)
    nt = (((1,), (1,)), ((), ()))
    scale2 = scale * LOG2E
    for h in range(hg):
        kk[0:blk, :] = kp_ref[h]
        kk[blk:2 * blk, :] = kc_ref[h]
        vv[0:blk, :] = vp_ref[h]
        vv[blk:2 * blk, :] = vc_ref[h]
        for g, (_, dil) in enumerate(DILATED_BRANCHES):

            def tile(t, carry, g=g, dil=dil):
                r = t % dil
                a = t // dil
                start = r + dil * qb * a
                q = q_ref[h, pl.ds(start, qb, stride=dil), :].astype(bf16)
                kstart = blk + start - dil * qb
                k = kk[pl.ds(kstart, 2 * qb, stride=dil), :].astype(bf16)
                v = vv[pl.ds(kstart, 2 * qb, stride=dil), :].astype(bf16)
                before_start = jnp.where((a > 0) | (c > 0), 0.0, NEG_BIG)
                bias = jnp.concatenate([bias_prev + before_start, bias_cur], axis=1)
                s = lax.dot_general(q, k, nt, preferred_element_type=f32) * scale2 + bias
                m = jnp.max(s, axis=-1, keepdims=True)
                p = jnp.exp2(s - m)
                l = jnp.sum(p, axis=-1, keepdims=True)
                o = jnp.dot(p.astype(bf16), v, preferred_element_type=f32) / l
                ob[g, pl.ds(start, qb, stride=dil), :] = o
                lb[g, pl.ds(start, qb, stride=dil), :] = jnp.broadcast_to((m + jnp.log2(l)) * LN2, (qb, LANES))
                return carry

            lax.fori_loop(0, blk // qb, tile, 0, unroll=True)
        step = 256
        for r0 in range(0, blk, step):
            ls = [lb[g, r0:r0 + step, :] for g in range(len(DILATED_BRANCHES))]
            mx = functools.reduce(jnp.maximum, ls)
            ws = [jnp.exp(x - mx) for x in ls]
            tot = functools.reduce(lambda x, y: x + y, ws)
            acc = ws[0] * ob[0, r0:r0 + step, :]
            for g in range(1, len(DILATED_BRANCHES)):
                acc = acc + ws[g] * ob[g, r0:r0 + step, :]
            o_ref[r0:r0 + step, h * HEAD_DIM:(h + 1) * HEAD_DIM] = acc / tot


def prompt_attention(q, k, v, seq):
    n_heads, m, _ = q.shape
    blk = MAX_DIL * ATTN_QB
    hg = 2
    assert seq % blk == 0 and n_heads % hg == 0 and m >= seq
    assert all(w == dil * ATTN_QB for w, dil in DILATED_BRANCHES)
    cur = pl.BlockSpec((hg, blk, HEAD_DIM), lambda i, c: (i, c, 0))
    prev = pl.BlockSpec((hg, blk, HEAD_DIM), lambda i, c: (i, jnp.maximum(c - 1, 0), 0))
    nbr = len(DILATED_BRANCHES)
    return pl.pallas_call(
        functools.partial(_prompt_attn_kernel, scale=1.0 / math.sqrt(HEAD_DIM)),
        out_shape=jax.ShapeDtypeStruct((seq, n_heads * HEAD_DIM), f32),
        grid=(n_heads // hg, seq // blk),
        in_specs=[cur, prev, cur, prev, cur],
        out_specs=pl.BlockSpec((blk, hg * HEAD_DIM), lambda i, c: (c, i)),
        scratch_shapes=[pltpu.VMEM((2 * blk, HEAD_DIM), f32), pltpu.VMEM((2 * blk, HEAD_DIM), f32),
                        pltpu.VMEM((nbr, blk, HEAD_DIM), f32), pltpu.VMEM((nbr, blk, LANES), f32)],
        compiler_params=_cparams(("parallel", "arbitrary"), 48),
        name="prompt_attn",
    )(q, k, k, v, v)


def _sample_key_positions(win_len, dec):
    half = MAX_DIL // 2
    ma, ra = np.meshgrid(np.arange(win_len // MAX_DIL), np.arange(half), indexing="ij")
    pos_a = (MAX_DIL * ma + ra).reshape(-1)
    mb0 = 3 * (win_len // MAX_DIL) // 4
    mb, rb = np.meshgrid(np.arange(mb0, win_len // MAX_DIL), np.arange(half, MAX_DIL), indexing="ij")
    pos_b = (MAX_DIL * mb + rb).reshape(-1)
    return np.concatenate([pos_a, pos_b, win_len + np.arange(dec)])


def _sample_bias(win_len, dec, n_heads):
    kpos = _sample_key_positions(win_len, dec)[:, None]
    qpos = PAST_LEN + (np.arange(LANES) % dec)[None, :]
    delta = qpos - kpos
    out = []
    for window, dil in DILATED_BRANCHES:
        ok = (delta >= 0) & (delta <= window) & (delta % dil == 0) & (kpos >= 0)
        out.append(np.where(ok, 0.0, NEG_BIG))
    out = np.stack(out).astype(np.float32)
    out[:, :, n_heads * dec:] = NEG_BIG
    return out


def _sample_attn_kernel(q_ref, kn_ref, vn_ref, ka_ref, kb_ref, va_ref, vb_ref, bias_ref, o_ref, *, scale):
    n_heads, dec, _ = q_ref.shape
    d = n_heads * HEAD_DIM
    nq = LANES

    def gather(ref):
        n = ref.shape[1] * ref.shape[2]
        return jnp.concatenate([ref[h].reshape(n, HEAD_DIM) for h in range(n_heads)], axis=-1).astype(bf16)

    def natural(ref):
        return jnp.concatenate([ref[h] for h in range(n_heads)], axis=-1)

    qt = jnp.concatenate([natural(q_ref)] * (nq // dec), axis=0)
    rowh = lax.broadcasted_iota(jnp.int32, (nq, d), 0) // dec
    colh = lax.broadcasted_iota(jnp.int32, (nq, d), 1) // HEAD_DIM
    qbd = jnp.where(rowh == colh, qt, 0.0).astype(bf16)
    nt = (((1,), (1,)), ((), ()))
    sa = lax.dot_general(gather(ka_ref), qbd, nt, preferred_element_type=f32)
    sb = lax.dot_general(gather(kb_ref), qbd, nt, preferred_element_type=f32)
    sn = lax.dot_general(natural(kn_ref).astype(bf16), qbd, nt, preferred_element_type=f32)
    na, nb = sa.shape[0], sb.shape[0]
    s = jnp.concatenate([sa, sb, sn], axis=0) * scale
    ps, ls, lses = [], [], []
    for g in range(len(DILATED_BRANCHES)):
        sg = s + bias_ref[g]
        m = jnp.max(sg, axis=0, keepdims=True)
        p = jnp.exp(sg - m)
        l = jnp.sum(p, axis=0, keepdims=True)
        ps.append(p)
        ls.append(l)
        lses.append(m + jnp.log(l))
    mx = functools.reduce(jnp.maximum, lses)
    ws = [jnp.exp(x - mx) for x in lses]
    tot = functools.reduce(lambda x, y: x + y, ws)
    pm = ps[0] * (ws[0] / (tot * ls[0]))
    for p, w, l in zip(ps[1:], ws[1:], ls[1:]):
        pm = pm + p * (w / (tot * l))
    pm = pm.astype(bf16)
    tn = (((0,), (0,)), ((), ()))
    o = lax.dot_general(pm[:na], gather(va_ref), tn, preferred_element_type=f32)
    o = o + lax.dot_general(pm[na:na + nb], gather(vb_ref), tn, preferred_element_type=f32)
    o = o + lax.dot_general(pm[na + nb:], natural(vn_ref).astype(bf16), tn, preferred_element_type=f32)
    for h in range(n_heads):
        hs = slice(h * HEAD_DIM, (h + 1) * HEAD_DIM)
        o_ref[:, hs] = o[h * dec:(h + 1) * dec, hs]


def sample_attention(q, k, v, cache_k, cache_v, row0):
    n_heads, m, _ = q.shape
    nbat, win_len = cache_k.shape[0], cache_k.shape[1]
    dec = (m - row0) // nbat
    assert win_len == PAST_LEN and dec == SUBLANES and row0 % dec == 0 and n_heads * dec <= LANES
    assert win_len % (4 * MAX_DIL) == 0 and DILATED_BRANCHES[1][0] <= win_len // 4
    half = MAX_DIL // 2
    mgrp = win_len // MAX_DIL
    d = n_heads * HEAD_DIM
    ck = cache_k.transpose(0, 2, 1, 3).reshape(nbat, n_heads, mgrp, MAX_DIL, HEAD_DIM)
    cv = cache_v.transpose(0, 2, 1, 3).reshape(nbat, n_heads, mgrp, MAX_DIL, HEAD_DIM)
    bias = jnp.asarray(_sample_bias(win_len, dec, n_heads))
    new = pl.BlockSpec((n_heads, dec, HEAD_DIM), lambda b: (0, row0 // dec + b, 0))
    part_a = pl.BlockSpec((None, n_heads, mgrp, half, HEAD_DIM), lambda b: (b, 0, 0, 0, 0))
    part_b = pl.BlockSpec((None, n_heads, mgrp // 4, half, HEAD_DIM), lambda b: (b, 0, 3, 1, 0))
    out = pl.pallas_call(
        functools.partial(_sample_attn_kernel, scale=1.0 / math.sqrt(HEAD_DIM)),
        out_shape=jax.ShapeDtypeStruct((nbat, dec, d), f32),
        grid=(nbat,),
        in_specs=[new, new, new, part_a, part_b, part_a, part_b, pl.BlockSpec(bias.shape, lambda b: (0, 0, 0))],
        out_specs=pl.BlockSpec((None, dec, d), lambda b: (b, 0, 0)),
        compiler_params=_cparams(("parallel",), 56),
        name="sample_attn",
    )(q, k, v, ck, ck, cv, cv, bias)
    return out.reshape(nbat * dec, d)


def _s5_discretise(a_re, a_im, dt):
    mag = jnp.exp(a_re * dt)
    lam_re = mag * jnp.cos(a_im * dt)
    lam_im = mag * jnp.sin(a_im * dt)
    num_re = lam_re - 1.0
    num_im = lam_im
    den = a_re * a_re + a_im * a_im
    f_re = (num_re * a_re + num_im * a_im) / den
    f_im = (num_im * a_re - num_re * a_im) / den
    return lam_re, lam_im, f_re, f_im


def _cmul(ar, ai, br, bi):
    return ar * br - ai * bi, ar * bi + ai * br


def _s5_operators_kernel(arc, aic, ldc, btr, bti, ccr, cci, arb, aib, ldb, cbr, cbi, ars, ais, lds,
                         w_ref, p_ref, q_ref, lr_ref, li_ref):
    t_len = SSM_T
    nt = (((1,), (1,)), ((), ()))
    hi = lax.Precision.HIGHEST
    gsz, ssz = SSM_GROUP, SSM_STATE

    def group_mask(shape, row_div, col_div):
        r = lax.broadcasted_iota(jnp.int32, shape, 0) // row_div
        c = lax.broadcasted_iota(jnp.int32, shape, 1) // col_div
        return r == c

    lr, li, fr, fi = _s5_discretise(arc[...], aic[...], jnp.exp(ldc[...]))
    er, ei = _cmul(fr, fi, btr[...], bti[...])
    c_re, c_im = ccr[...], cci[...]
    mask_w = group_mask((LANES, LANES), gsz, gsz)
    mask_p = group_mask((LANES, STATE_BLOCK), gsz, ssz)
    zero_blk = jnp.zeros((LANES, LANES), bf16)
    d_blocks, e_pows = [], []
    for n in range(t_len):
        e_pows.append((er, ei))
        dn = (lax.dot_general(er, c_re, nt, precision=hi, preferred_element_type=f32)
              - lax.dot_general(ei, c_im, nt, precision=hi, preferred_element_type=f32))
        d_blocks.append(jnp.where(mask_w, dn, 0.0).astype(bf16))
        er, ei = _cmul(lr, li, er, ei)
    for ti in range(t_len):
        for to in range(t_len):
            w_ref[ti * LANES:(ti + 1) * LANES, to * LANES:(to + 1) * LANES] = (
                d_blocks[to - ti] if to >= ti else zero_blk)
        pr, pi = e_pows[t_len - 1 - ti]
        p_ref[ti * LANES:(ti + 1) * LANES, 0:STATE_BLOCK] = jnp.where(
            mask_p, jnp.concatenate([pr] * GROUPS_PER_BLOCK, axis=1), 0.0).astype(bf16)
        p_ref[ti * LANES:(ti + 1) * LANES, STATE_BLOCK:] = jnp.where(
            mask_p, jnp.concatenate([pi] * GROUPS_PER_BLOCK, axis=1), 0.0).astype(bf16)
    lbr, lbi, _, _ = _s5_discretise(arb[...], aib[...], jnp.exp(ldb[...]))
    mask_q = group_mask((STATE_BLOCK, LANES), ssz, gsz)
    qr, qi = cbr[...], cbi[...]
    for t in range(t_len):
        qr, qi = _cmul(qr, qi, lbr, lbi)
        q_ref[0:STATE_BLOCK, t * LANES:(t + 1) * LANES] = jnp.where(
            mask_q, jnp.concatenate([qr] * GROUPS_PER_BLOCK, axis=0), 0.0).astype(bf16)
        q_ref[STATE_BLOCK:, t * LANES:(t + 1) * LANES] = jnp.where(
            mask_q, jnp.concatenate([-qi] * GROUPS_PER_BLOCK, axis=0), 0.0).astype(bf16)
    lsr, lsi, _, _ = _s5_discretise(ars[...], ais[...], jnp.exp(lds[...]))
    pr, pi = lsr, lsi
    for _ in range(t_len - 1):
        pr, pi = _cmul(pr, pi, lsr, lsi)
    lr_ref[...] = pr
    li_ref[...] = pi


def s5_chunk_operators(a_re, a_im, log_dt, b_re, b_im, c_re, c_im):
    g, p = a_re.shape
    ch = b_re.shape[2]
    j = GROUPS_PER_BLOCK
    nb = g // j
    assert p == SSM_STATE and ch == SSM_GROUP and g % j == 0
    t = SSM_T
    a_re, a_im, log_dt, b_re, b_im, c_re, c_im = (x.astype(f32) for x in (a_re, a_im, log_dt, b_re, b_im, c_re, c_im))
    by_row = lambda x: jnp.repeat(x.reshape(nb, j, p), ch, axis=1)
    by_col = lambda x: jnp.repeat(x.reshape(nb, j, p).transpose(0, 2, 1), ch, axis=2)
    ld = log_dt.reshape(nb, j)
    ld_row = jnp.repeat(ld, ch, axis=1)[:, :, None]
    ld_col = jnp.repeat(ld, ch, axis=1)[:, None, :]
    ld_flat = jnp.repeat(ld, p, axis=1)[:, None, :]
    bt = lambda x: x.reshape(nb, j, p, ch).transpose(0, 1, 3, 2).reshape(nb, j * ch, p)
    c_row = lambda x: x.reshape(nb, j * ch, p)
    c_col = lambda x: x.reshape(nb, j, ch, p).transpose(0, 3, 1, 2).reshape(nb, p, j * ch)
    flat = lambda x: x.reshape(nb, 1, j * p)
    ops = [by_row(a_re), by_row(a_im), ld_row, bt(b_re), bt(b_im), c_row(c_re), c_row(c_im),
           by_col(a_re), by_col(a_im), ld_col, c_col(c_re), c_col(c_im),
           flat(a_re), flat(a_im), ld_flat]
    blk = lambda arr: pl.BlockSpec((None,) + arr.shape[1:], lambda i: (i, 0, 0))
    shp = jax.ShapeDtypeStruct
    sb = STATE_BLOCK
    outs = [shp((nb, t * LANES, t * LANES), bf16), shp((nb, t * LANES, 2 * sb), bf16),
            shp((nb, 2 * sb, t * LANES), bf16), shp((nb, 1, sb), f32), shp((nb, 1, sb), f32)]
    return pl.pallas_call(
        _s5_operators_kernel,
        out_shape=outs,
        grid=(nb,),
        in_specs=[blk(x) for x in ops],
        out_specs=[blk(x) for x in outs],
        compiler_params=_cparams(("parallel",), 32),
        name="s5_operators",
    )(*ops)


def _s5_kernel(u_ref, w_ref, p_ref, q_ref, lr_ref, li_ref, dsk_ref, h0r_ref, h0i_ref,
               gl_ref, fpr_ref, fpi_ref, fsr_ref, fsi_ref, h_scr, *, segs_per_dot):
    seg, pitch = S5_SEG, S5_SEG_PITCH
    n_seg = u_ref.shape[0] // (SSM_T * seg)
    n_pseg = n_seg - 1
    nsl = 2 * STATE_BLOCK // LANES
    half = nsl // 2
    lr, li = lr_ref[...], li_ref[...]
    slab = lambda x, l: x[:, l * LANES:(l + 1) * LANES]

    def ucat(s0):
        r0, n = s0 * seg, segs_per_dot * seg
        return jnp.concatenate([u_ref[pl.ds(SSM_T * r0 + t, n, stride=SSM_T), :] for t in range(SSM_T)], axis=-1)

    def seg_rows(s):
        return slice(s * pitch, s * pitch + seg)

    for s0 in range(0, n_seg, segs_per_dot):
        res = jnp.dot(ucat(s0).astype(bf16), p_ref[...], preferred_element_type=f32)
        for s in range(segs_per_dot):
            for l in range(nsl):
                h_scr[l, seg_rows(s0 + s), :] = slab(res[s * seg:(s + 1) * seg], l)

    lam_r = [jnp.broadcast_to(slab(lr, l), (SUBLANES, LANES)) for l in range(half)]
    lam_i = [jnp.broadcast_to(slab(li, l), (SUBLANES, LANES)) for l in range(half)]

    def load(k):
        return [h_scr[l, pl.ds(k, SUBLANES, stride=pitch), :] for l in range(nsl)]

    def advance(c, x, mr, mi):
        re = [mr[l] * c[l] - mi[l] * c[l + half] + x[l] for l in range(half)]
        im = [mr[l] * c[l + half] + mi[l] * c[l] + x[l + half] for l in range(half)]
        return tuple(re + im)

    zeros = tuple(jnp.zeros((SUBLANES, LANES), f32) for _ in range(nsl))
    ends = lax.fori_loop(0, seg, lambda k, c: advance(c, load(k), lam_r, lam_i), zeros, unroll=8)

    pr, pi = [slab(lr, l) for l in range(half)], [slab(li, l) for l in range(half)]
    for _ in range(seg.bit_length() - 1):
        sq = [_cmul(a, b, a, b) for a, b in zip(pr, pi)]
        pr, pi = [x[0] for x in sq], [x[1] for x in sq]
    cur = tuple(jnp.zeros((1, LANES), f32) for _ in range(nsl))
    starts = [cur]
    for j in range(n_pseg):
        cur = advance(cur, [e[j:j + 1, :] for e in ends], pr, pi)
        starts.append(cur)
    fpr_ref[...] = jnp.concatenate(starts[n_pseg][:half], axis=-1)
    fpi_ref[...] = jnp.concatenate(starts[n_pseg][half:], axis=-1)
    init = tuple(jnp.concatenate([starts[j][l] for j in range(n_pseg)], axis=0) for l in range(nsl))

    def sweep(k, c):
        x = load(k)
        for l in range(nsl):
            h_scr[l, pl.ds(k, SUBLANES, stride=pitch), :] = c[l]
        return advance(c, x, lam_r, lam_i)

    lax.fori_loop(0, seg, sweep, init, unroll=8)

    h0r, h0i = h0r_ref[...], h0i_ref[...]
    last = seg_rows(n_pseg)
    inr = jnp.concatenate([h_scr[l, last, :] for l in range(half)], axis=-1)
    ini = jnp.concatenate([h_scr[l + half, last, :] for l in range(half)], axis=-1)
    fsr_ref[...] = lr * h0r - li * h0i + inr
    fsi_ref[...] = lr * h0i + li * h0r + ini
    for l in range(half):
        h_scr[l, last, :] = slab(h0r, l)
        h_scr[l + half, last, :] = slab(h0i, l)

    dsk = dsk_ref[...]
    for s0 in range(0, n_seg, segs_per_dot):
        u = ucat(s0)
        hp = jnp.concatenate([jnp.concatenate([h_scr[l, seg_rows(s0 + s), :] for l in range(nsl)], axis=-1)
                              for s in range(segs_per_dot)], axis=0)
        y = jnp.dot(u.astype(bf16), w_ref[...], preferred_element_type=f32)
        y = y + jnp.dot(hp.astype(bf16), q_ref[...], preferred_element_type=f32)
        gl = jax.nn.gelu(y + dsk * u)
        r0, n = s0 * seg, segs_per_dot * seg
        for t in range(SSM_T):
            gl_ref[pl.ds(SSM_T * r0 + t, n, stride=SSM_T), :] = gl[:, t * LANES:(t + 1) * LANES]


def s5_mixer(u, ops, d_skip, h0_re, h0_im, n_prompt):
    w, pm, qm, lr, li = ops
    m, dssm = u.shape
    nb = dssm // LANES
    t = SSM_T
    nbat = h0_re.shape[0]
    n_seg = m // (t * S5_SEG)
    segs_per_dot = 3
    assert n_prompt == SUBLANES * S5_SEG * t and nbat == S5_SEG and m == n_prompt + nbat * t
    assert n_seg % segs_per_dot == 0
    sb = STATE_BLOCK
    dsk = jnp.tile(d_skip.astype(f32).reshape(nb, 1, LANES), (1, 1, t))
    h0r = h0_re.astype(f32).reshape(nbat, nb * sb)
    h0i = h0_im.astype(f32).reshape(nbat, nb * sb)
    blk = lambda shape: pl.BlockSpec((None,) + shape, lambda i: (i, 0, 0))
    ublk = pl.BlockSpec((m, LANES), lambda i: (0, i))
    hblk = pl.BlockSpec((nbat, sb), lambda i: (0, i))
    fblk = pl.BlockSpec((1, sb), lambda i: (0, i))
    shp = jax.ShapeDtypeStruct
    return pl.pallas_call(
        functools.partial(_s5_kernel, segs_per_dot=segs_per_dot),
        out_shape=[shp((m, dssm), f32), shp((1, nb * sb), f32), shp((1, nb * sb), f32),
                   shp((nbat, nb * sb), f32), shp((nbat, nb * sb), f32)],
        grid=(nb,),
        in_specs=[ublk, blk((t * LANES, t * LANES)), blk((t * LANES, 2 * sb)), blk((2 * sb, t * LANES)),
                  blk((1, sb)), blk((1, sb)), blk((1, t * LANES)), hblk, hblk],
        out_specs=[ublk, fblk, fblk, hblk, hblk],
        scratch_shapes=[pltpu.VMEM((2 * sb // LANES, n_seg * S5_SEG_PITCH, LANES), f32)],
        compiler_params=_cparams(("parallel",), 56),
        name="s5_mixer",
    )(u, w, pm, qm, lr, li, dsk, h0r, h0i)


def _cross_attn_cache_kernel(q_ref, k_ref, v_ref, o_ref, *, scale):
    gb, n, nh, hd = k_ref.shape
    dec = q_ref.shape[1]
    rowh = lax.broadcasted_iota(jnp.int32, (n * nh, nh * dec), 0) % nh
    colh = lax.broadcasted_iota(jnp.int32, (n * nh, nh * dec), 1) // dec
    same = rowh == colh
    for g in range(gb):
        kf = k_ref[g].reshape(n * nh, hd).astype(bf16)
        vf = v_ref[g].reshape(n * nh, hd).astype(bf16)
        q = q_ref[g]
        qs = jnp.concatenate([q[:, h * hd:(h + 1) * hd] for h in range(nh)], axis=0).astype(bf16)
        s = lax.dot_general(kf, qs, (((1,), (1,)), ((), ())), preferred_element_type=f32) * scale
        s = jnp.where(same, s, NEG_BIG)
        m = jnp.max(s, axis=0, keepdims=True)
        p = jnp.exp(s - m)
        p = p / jnp.sum(p, axis=0, keepdims=True)
        o = lax.dot_general(p.astype(bf16), vf, (((0,), (0,)), ((), ())), preferred_element_type=f32)
        for h in range(nh):
            o_ref[g, :, h * hd:(h + 1) * hd] = o[h * dec:(h + 1) * dec, :].astype(o_ref.dtype)


def _cross_attn_kernel(q_ref, k_ref, v_ref, o_ref, *, scale):
    q = q_ref[...].astype(bf16)
    for h in range(MEM_HEADS):
        hs = slice(h * MEM_HEAD_DIM, (h + 1) * MEM_HEAD_DIM)
        k = k_ref[:, :, hs].astype(bf16)
        v = v_ref[:, :, hs].astype(bf16)
        s = jnp.einsum("gqd,gkd->gqk", q[:, :, hs], k, preferred_element_type=f32) * scale
        m = jnp.max(s, axis=-1, keepdims=True)
        p = jnp.exp(s - m)
        p = p / jnp.sum(p, axis=-1, keepdims=True)
        o = jnp.einsum("gqk,gkd->gqd", p.astype(bf16), v, preferred_element_type=f32)
        o_ref[:, :, hs] = o.astype(o_ref.dtype)


def cross_attention(q3, k, v, nq, gb, tq, q_block0, name):
    g = k.shape[0]
    e = q3.shape[2]
    kblock = (gb,) + k.shape[1:]
    kmap = (lambda i, j: (i, 0, 0)) if k.ndim == 3 else (lambda i, j: (i, 0, 0, 0))
    body = _cross_attn_kernel if k.ndim == 3 else _cross_attn_cache_kernel
    return pl.pallas_call(
        functools.partial(body, scale=1.0 / math.sqrt(MEM_HEAD_DIM)),
        out_shape=jax.ShapeDtypeStruct((g, nq, e), bf16),
        grid=(g // gb, nq // tq),
        in_specs=[pl.BlockSpec((gb, tq, e), lambda i, j: (q_block0 + i, j, 0)),
                  pl.BlockSpec(kblock, kmap), pl.BlockSpec(kblock, kmap)],
        out_specs=pl.BlockSpec((gb, tq, e), lambda i, j: (i, j, 0)),
        compiler_params=_cparams(("parallel", "parallel"), 48),
        name=name,
    )(q3, k, v)


def _rotary_tables(pos):
    inv = 1.0 / (ROPE_THETA ** (jnp.arange(0, HEAD_DIM, 2, dtype=f32) / HEAD_DIM))
    ang = pos.astype(f32)[:, None] * inv[None, :]
    cos, sin = jnp.cos(ang), jnp.sin(ang)
    return jnp.concatenate([cos, cos], axis=-1), jnp.concatenate([-sin, sin], axis=-1)


def kernel(x_prompt, x_sample, cache_win_k, cache_win_v, state_ssm_re, state_ssm_im, cache_mem_k, cache_mem_v,
           mem_prompt, g_mix, w_in, a_re, a_im, log_dt, b_re, b_im, c_re, c_im, d_skip, w_glu, b_glu,
           g_attn_out, g_ssm_out, w_out, g_cross, g_mem, w_mq, w_mk, w_mv, w_mo, g_ffn, w_up, w_down, g_final):
    bp, sp, dm = x_prompt.shape
    nbat, dec, _ = x_sample.shape
    assert bp == 1
    ns = nbat * dec
    n_groups = a_re.shape[0]
    d_ssm = n_groups * SSM_GROUP
    d_attn = dm - d_ssm
    n_heads = d_attn // HEAD_DIM
    e_mem = MEM_HEADS * MEM_HEAD_DIM
    n_mem = mem_prompt.shape[1]
    keep = min(PAST_LEN, sp)

    xp, xs = x_prompt.reshape(sp, dm), x_sample.reshape(ns, dm)
    mtot = sp + ns
    pos = jnp.concatenate([jnp.arange(sp, dtype=jnp.int32), PAST_LEN + jnp.tile(jnp.arange(dec, dtype=jnp.int32), nbat)])
    cos2, sin2 = _rotary_tables(pos)
    wb = lambda w_: w_.astype(bf16)

    h = rmsnorm2(xp, xs, g_mix, bf16)
    rot =((cos2, "rowtab"), (sin2, "rowtab"))
    w_in_b = wb(w_in)
    q = matmul(h, w_in_b, _ep_rotary, rot, head_major=True, cols=(0, d_attn), name="proj_q")
    k = matmul(h, w_in_b, _ep_rotary, rot, head_major=True, cols=(d_attn, d_attn), name="proj_k")
    v = matmul(h, w_in_b, head_major=True, cols=(2 * d_attn, d_attn), name="proj_v")
    u = matmul(h, w_in_b, cols=(3 * d_attn, d_ssm), name="proj_u")

    attn_p = prompt_attention(q, k, v, sp)
    attn_s = sample_attention(q, k, v, cache_win_k, cache_win_v, sp)
    na = rmsnorm2(attn_p, attn_s, g_attn_out, bf16)

    ops = s5_chunk_operators(a_re, a_im, log_dt, b_re, b_im, c_re, c_im)
    gl, fpr, fpi, fsr, fsi = s5_mixer(u, ops, d_skip, state_ssm_re, state_ssm_im, sp)
    y_ssm = matmul(gl, wb(w_glu), _ep_glu, ((gl, "tile"), (b_glu.astype(f32).reshape(1, d_ssm), "col")), tn_cap=512,
                   name="glu")
    ny = rmsnorm(y_ssm, g_ssm_out, bf16)

    x1 = out_projection(na, ny, wb(w_out[:d_attn]), wb(w_out[d_attn:]), xp, xs)

    mem_n = rmsnorm(mem_prompt.reshape(n_mem, dm), g_mem, bf16)
    mem_k_p = matmul(mem_n, wb(w_mk), name="mem_k")
    mem_v_p = matmul(mem_n, wb(w_mv), name="mem_v")
    qm = norm_projection(x1, g_cross, wb(w_mq))
    o_p = cross_attention(qm.reshape(1, mtot, e_mem), mem_k_p.reshape(1, n_mem, e_mem), mem_v_p.reshape(1, n_mem, e_mem),
                          sp, 1, 512, 0, "cross_attn_prompt")
    o_s = cross_attention(qm.reshape(mtot // dec, dec, e_mem), cache_mem_k, cache_mem_v,
                          dec, 8, dec, sp // dec // 8, "cross_attn_sample")
    o_c = jnp.concatenate([o_p.reshape(sp, e_mem), o_s.reshape(ns, e_mem)], axis=0)
    x2, hf = projection_norm(o_c, wb(w_mo), x1, g_ffn)

    hid = matmul(hf, w_up, _ep_relu2, out_dtype=bf16, tm_pref=1536, tn_cap=512, name="ffn_up")
    x3 = matmul(hid, wb(w_down), _ep_residual, ((x2, "tile"),), tm_pref=1536, name="ffn_down")
    y_p = rmsnorm(x3, g_final, f32, 0, sp)
    y_s = rmsnorm(x3, g_final, f32, sp, ns)

    hd = (n_heads, HEAD_DIM)
    natural = lambda t, r0, r1: t[:, r0:r1].transpose(1, 0, 2)
    return (y_p.reshape(1, sp, dm), y_s.reshape(nbat, dec, dm),
            natural(k, sp - keep, sp).reshape(1, keep, *hd), natural(v, sp - keep, sp).reshape(1, keep, *hd),
            fpr.reshape(1, n_groups, SSM_STATE), fpi.reshape(1, n_groups, SSM_STATE),
            mem_k_p.reshape(1, n_mem, MEM_HEADS, MEM_HEAD_DIM), mem_v_p.reshape(1, n_mem, MEM_HEADS, MEM_HEAD_DIM),
            natural(k, sp, mtot).reshape(nbat, dec, *hd), natural(v, sp, mtot).reshape(nbat, dec, *hd),
            fsr.reshape(nbat, n_groups, SSM_STATE), fsi.reshape(nbat, n_groups, SSM_STATE))
```

```python
import functools
import math

import numpy as np
import jax
import jax.numpy as jnp
from jax import lax
from jax.experimental import pallas as pl
from jax.experimental.pallas import tpu as pltpu

f32 = jnp.float32
bf16 = jnp.bfloat16

HEAD_DIM = 128
SSM_GROUP = 16
SSM_STATE = 64
DILATED_BRANCHES = ((128, 1), (512, 4), (2048, 16))
MAX_DIL = 16
PAST_LEN = 2048
MEM_HEADS = 4
MEM_HEAD_DIM = 128
ROPE_THETA = 10000.0
NORM_EPS = 1e-6
NEG_BIG = -1e30
LOG2E = math.log2(math.e)
LN2 = math.log(2.0)

LANES = 128
SUBLANES = 8
SSM_T = 8
GROUPS_PER_BLOCK = LANES // SSM_GROUP
STATE_BLOCK = GROUPS_PER_BLOCK * SSM_STATE
ATTN_QB = 128
S5_SEG = 128
S5_SEG_PITCH = S5_SEG + SUBLANES
V7X_VMEM_BYTES = 64 * 1024 * 1024


def _cparams(sem, vmem_mb):
    assert vmem_mb * 1024 * 1024 < V7X_VMEM_BYTES
    return pltpu.CompilerParams(dimension_semantics=sem, vmem_limit_bytes=vmem_mb * 1024 * 1024)


def _rmsnorm_kernel(x_ref, g_ref, o_ref):
    x = x_ref[...].astype(f32)
    y = x * lax.rsqrt(jnp.mean(x * x, axis=-1, keepdims=True) + NORM_EPS)
    o_ref[...] = (y * g_ref[...]).astype(o_ref.dtype)


def rmsnorm(x, g, out_dtype, row_start=0, rows=None, tm=256):
    m, d = x.shape
    rows = m if rows is None else rows
    tm = min(tm, rows)
    assert rows % tm == 0 and row_start % tm == 0
    off = row_start // tm
    return pl.pallas_call(
        _rmsnorm_kernel,
        out_shape=jax.ShapeDtypeStruct((rows, d), out_dtype),
        grid=(rows // tm,),
        in_specs=[pl.BlockSpec((tm, d), lambda i: (i + off, 0)),
                  pl.BlockSpec((1, d), lambda i: (0, 0))],
        out_specs=pl.BlockSpec((tm, d), lambda i: (i, 0)),
        compiler_params=_cparams(("parallel",), 40),
        name="rmsnorm",
    )(x, g.reshape(1, d).astype(f32))


def _rmsnorm2_kernel(xa_ref, xb_ref, g_ref, o_ref, *, na):
    i = pl.program_id(0)

    @pl.when(i < na)
    def _():
        _rmsnorm_kernel(xa_ref, g_ref, o_ref)

    @pl.when(i >= na)
    def _():
        _rmsnorm_kernel(xb_ref, g_ref, o_ref)


def rmsnorm2(xa, xb, g, out_dtype, tm=256):
    (ma, d), (mb, _) = xa.shape, xb.shape
    assert ma % tm == 0 and mb % tm == 0
    na = ma // tm
    return pl.pallas_call(
        functools.partial(_rmsnorm2_kernel, na=na),
        out_shape=jax.ShapeDtypeStruct((ma + mb, d), out_dtype),
        grid=((ma + mb) // tm,),
        in_specs=[pl.BlockSpec((tm, d), lambda i: (jnp.minimum(i, na - 1), 0)),
                  pl.BlockSpec((tm, d), lambda i: (jnp.maximum(i - na, 0), 0)),
                  pl.BlockSpec((1, d), lambda i: (0, 0))],
        out_specs=pl.BlockSpec((tm, d), lambda i: (i, 0)),
        compiler_params=_cparams(("arbitrary",), 40),
        name="rmsnorm2",
    )(xa, xb, g.reshape(1, d).astype(f32))


def _ep_none(acc):
    return acc


def _ep_residual(acc, res):
    return acc + res


def _ep_relu2(acc):
    r = jnp.maximum(acc, 0.0)
    return r * r


def _ep_rotary(acc, cos2, sin2):
    parts = []
    for h in range(acc.shape[1] // HEAD_DIM):
        blk = acc[:, h * HEAD_DIM:(h + 1) * HEAD_DIM]
        parts.append(blk * cos2 + pltpu.roll(blk, HEAD_DIM // 2, 1) * sin2)
    return jnp.concatenate(parts, axis=-1)


def _mm_kernel(*refs, nk, epilogue, n_extra, head_major):
    a_ref, b_ref = refs[0], refs[1]
    extra = refs[2:2 + n_extra]
    o_ref = refs[2 + n_extra]

    def prod():
        return jnp.dot(a_ref[...].astype(bf16), b_ref[...].astype(bf16), preferred_element_type=f32)

    def finish(acc):
        res = epilogue(acc, *[e[...] for e in extra]).astype(o_ref.dtype)
        if head_major:
            for h in range(o_ref.shape[0]):
                o_ref[h] = res[:, h * LANES:(h + 1) * LANES]
        else:
            o_ref[...] = res

    if nk == 1:
        finish(prod())
        return
    k = pl.program_id(2)

    @pl.when(k == 0)
    def _():
        o_ref[...] = prod()

    @pl.when((k > 0) & (k < nk - 1))
    def _():
        o_ref[...] += prod()

    @pl.when(k == nk - 1)
    def _():
        finish(o_ref[...] + prod())


def _largest_tile(n, cap):
    best = None
    for t in range(LANES, cap + 1, LANES):
        if n % t == 0:
            best = t
    assert best is not None, (n, cap)
    return best


def matmul(a, b, epilogue=_ep_none, extras=(), out_dtype=f32, head_major=False, cols=None, tm_pref=1024,
           tn_cap=1024, name="matmul"):
    m, kdim = a.shape
    col0, n = (0, b.shape[1]) if cols is None else cols
    tm = next((t for t in (tm_pref, 1024) if m % t == 0), m)
    tn = _largest_tile(math.gcd(n, col0) if col0 else n, tn_cap)
    tk = kdim if kdim <= 4096 else 2048
    assert m % tm == 0 and kdim % tk == 0 and col0 % tn == 0 and n % tn == 0
    nk = kdim // tk
    assert nk == 1 or (out_dtype == f32 and not head_major)
    joff = col0 // tn
    in_specs = [pl.BlockSpec((tm, tk), lambda i, j, k: (i, k)),
                pl.BlockSpec((tk, tn), lambda i, j, k: (k, j + joff))]
    ops = [a, b]
    for arr, kind in extras:
        if kind == "tile":
            in_specs.append(pl.BlockSpec((tm, tn), lambda i, j, k: (i, j)))
        elif kind == "col":
            in_specs.append(pl.BlockSpec((1, tn), lambda i, j, k: (0, j)))
        else:
            in_specs.append(pl.BlockSpec((tm, arr.shape[1]), lambda i, j, k: (i, 0)))
        ops.append(arr)
    if head_major:
        out_shape = jax.ShapeDtypeStruct((n // LANES, m, LANES), out_dtype)
        out_spec = pl.BlockSpec((tn // LANES, tm, LANES), lambda i, j, k: (j, i, 0))
    else:
        out_shape = jax.ShapeDtypeStruct((m, n), out_dtype)
        out_spec = pl.BlockSpec((tm, tn), lambda i, j, k: (i, j))
    return pl.pallas_call(
        functools.partial(_mm_kernel, nk=nk, epilogue=epilogue, n_extra=len(extras), head_major=head_major),
        out_shape=out_shape,
        grid=(m // tm, n // tn, nk),
        in_specs=in_specs,
        out_specs=out_spec,
        compiler_params=_cparams(("parallel", "parallel", "arbitrary"), 60),
        name=name,
    )(*ops)


def _out_proj_kernel(na_ref, ny_ref, wa_ref, wy_ref, xa_ref, xb_ref, o_ref, *, nblk_a):
    i = pl.program_id(0)

    def with_residual(x_ref):
        acc = jnp.dot(na_ref[...], wa_ref[...], preferred_element_type=f32)
        acc = acc + jnp.dot(ny_ref[...], wy_ref[...], preferred_element_type=f32)
        o_ref[...] = acc + x_ref[...]

    @pl.when(i < nblk_a)
    def _():
        with_residual(xa_ref)

    @pl.when(i >= nblk_a)
    def _():
        with_residual(xb_ref)


def out_projection(na, ny, w_a, w_y, xa, xb):
    m, ka = na.shape
    ky = ny.shape[1]
    n = w_a.shape[1]
    tm = 1024
    tn = _largest_tile(n, 1024)
    assert m % tm == 0 and xa.shape[0] % tm == 0 and xa.shape[0] + xb.shape[0] == m
    nblk_a = xa.shape[0] // tm
    last_j = n // tn - 1
    xa_spec = pl.BlockSpec((tm, tn), lambda i, j: (jnp.minimum(i, nblk_a - 1), jnp.where(i < nblk_a, j, last_j)))
    xb_spec = pl.BlockSpec((tm, tn), lambda i, j: (jnp.maximum(i - nblk_a, 0), jnp.where(i < nblk_a, 0, j)))
    return pl.pallas_call(
        functools.partial(_out_proj_kernel, nblk_a=nblk_a),
        out_shape=jax.ShapeDtypeStruct((m, n), f32),
        grid=(m // tm, n // tn),
        in_specs=[pl.BlockSpec((tm, ka), lambda i, j: (i, 0)), pl.BlockSpec((tm, ky), lambda i, j: (i, 0)),
                  pl.BlockSpec((ka, tn), lambda i, j: (0, j)), pl.BlockSpec((ky, tn), lambda i, j: (0, j)),
                  xa_spec, xb_spec],
        out_specs=pl.BlockSpec((tm, tn), lambda i, j: (i, j)),
        compiler_params=_cparams(("arbitrary", "arbitrary"), 60),
        name="out_proj",
    )(na, ny, w_a, w_y, xa, xb)


def _norm_rows(x, g):
    return x * lax.rsqrt(jnp.mean(x * x, axis=-1, keepdims=True) + NORM_EPS) * g


def _norm_proj_kernel(x_ref, g_ref, w_ref, o_ref):
    h = _norm_rows(x_ref[...], g_ref[...]).astype(bf16)
    o_ref[...] = jnp.dot(h, w_ref[...], preferred_element_type=f32)


def norm_projection(x, g, w):
    m, d = x.shape
    n = w.shape[1]
    tm = 512
    assert m % tm == 0
    return pl.pallas_call(
        _norm_proj_kernel,
        out_shape=jax.ShapeDtypeStruct((m, n), f32),
        grid=(m // tm,),
        in_specs=[pl.BlockSpec((tm, d), lambda i: (i, 0)), pl.BlockSpec((1, d), lambda i: (0, 0)),
                  pl.BlockSpec((d, n), lambda i: (0, 0))],
        out_specs=pl.BlockSpec((tm, n), lambda i: (i, 0)),
        compiler_params=_cparams(("parallel",), 48),
        name="norm_proj",
    )(x, g.reshape(1, d).astype(f32), w)


def _proj_norm_kernel(a_ref, w_ref, res_ref, g_ref, x_ref, h_ref):
    x = res_ref[...] + jnp.dot(a_ref[...], w_ref[...], preferred_element_type=f32)
    x_ref[...] = x
    h_ref[...] = _norm_rows(x, g_ref[...]).astype(h_ref.dtype)


def projection_norm(a, w, res, g):
    m, kdim = a.shape
    n = w.shape[1]
    tm = 512
    assert m % tm == 0
    return pl.pallas_call(
        _proj_norm_kernel,
        out_shape=[jax.ShapeDtypeStruct((m, n), f32), jax.ShapeDtypeStruct((m, n), bf16)],
        grid=(m // tm,),
        in_specs=[pl.BlockSpec((tm, kdim), lambda i: (i, 0)), pl.BlockSpec((kdim, n), lambda i: (0, 0)),
                  pl.BlockSpec((tm, n), lambda i: (i, 0)), pl.BlockSpec((1, n), lambda i: (0, 0))],
        out_specs=[pl.BlockSpec((tm, n), lambda i: (i, 0)), pl.BlockSpec((tm, n), lambda i: (i, 0))],
        compiler_params=_cparams(("parallel",), 56),
        name="proj_norm",
    )(a, w, res, g.reshape(1, n).astype(f32))


def _glu_norm_kernel(gl_ref, w_ref, b_ref, g_ref, o_ref):
    gl = gl_ref[...]
    acc = jnp.dot(gl.astype(bf16), w_ref[...], preferred_element_type=f32)
    y = gl * jax.nn.sigmoid(acc + b_ref[...])
    o_ref[...] = _norm_rows(y, g_ref[...]).astype(o_ref.dtype)


def glu_norm(gl, w, bias, g):
    m, d = gl.shape
    tm = 512
    assert m % tm == 0 and w.shape == (d, d)
    row = pl.BlockSpec((1, d), lambda i: (0, 0))
    return pl.pallas_call(
        _glu_norm_kernel,
        out_shape=jax.ShapeDtypeStruct((m, d), bf16),
        grid=(m // tm,),
        in_specs=[pl.BlockSpec((tm, d), lambda i: (i, 0)), pl.BlockSpec((d, d), lambda i: (0, 0)), row, row],
        out_specs=pl.BlockSpec((tm, d), lambda i: (i, 0)),
        compiler_params=_cparams(("parallel",), 60),
        name="glu_norm",
    )(gl, w, bias.astype(f32).reshape(1, d), g.astype(f32).reshape(1, d))


def _prompt_attn_kernel(q_ref, kp_ref, kc_ref, vp_ref, vc_ref, o_ref, kk, vv, ob, lb, *, scale):
    c = pl.program_id(1)
    hg, blk, _ = q_ref.shape
    qb = ATTN_QB
    rows = lax.broadcasted_iota(jnp.int32, (qb, qb), 0)
    cols = lax.broadcasted_iota(jnp.int32, (qb, qb), 1)
    bias_prev = jnp.where(cols >= rows, 0.0, NEG_BIG)
    bias_cur = jnp.where(cols <= rows, 0.0, NEG_BIG)
    nt = (((1,), (1,)), ((), ()))
    scale2 = scale * LOG2E
    for h in range(hg):
        kk[0:blk, :] = kp_ref[h]
        kk[blk:2 * blk, :] = kc_ref[h]
        vv[0:blk, :] = vp_ref[h]
        vv[blk:2 * blk, :] = vc_ref[h]
        for g, (_, dil) in enumerate(DILATED_BRANCHES):

            def tile(t, carry, g=g, dil=dil):
                r = t % dil
                a = t // dil
                start = r + dil * qb * a
                q = q_ref[h, pl.ds(start, qb, stride=dil), :].astype(bf16)
                kstart = blk + start - dil * qb
                k = kk[pl.ds(kstart, 2 * qb, stride=dil), :].astype(bf16)
                v = vv[pl.ds(kstart, 2 * qb, stride=dil), :].astype(bf16)
                before_start = jnp.where((a > 0) | (c > 0), 0.0, NEG_BIG)
                bias = jnp.concatenate([bias_prev + before_start, bias_cur], axis=1)
                s = lax.dot_general(q, k, nt, preferred_element_type=f32) * scale2 + bias
                m = jnp.max(s, axis=-1, keepdims=True)
                p = jnp.exp2(s - m)
                l = jnp.sum(p, axis=-1, keepdims=True)
                o = jnp.dot(p.astype(bf16), v, preferred_element_type=f32) / l
                ob[g, pl.ds(start, qb, stride=dil), :] = o
                lb[g, pl.ds(start, qb, stride=dil), :] = jnp.broadcast_to((m + jnp.log2(l)) * LN2, (qb, LANES))
                return carry

            lax.fori_loop(0, blk // qb, tile, 0, unroll=True)
        step = 256
        for r0 in range(0, blk, step):
            ls = [lb[g, r0:r0 + step, :] for g in range(len(DILATED_BRANCHES))]
            mx = functools.reduce(jnp.maximum, ls)
            ws = [jnp.exp(x - mx) for x in ls]
            tot = functools.reduce(lambda x, y: x + y, ws)
            acc = ws[0] * ob[0, r0:r0 + step, :]
            for g in range(1, len(DILATED_BRANCHES)):
                acc = acc + ws[g] * ob[g, r0:r0 + step, :]
            o_ref[r0:r0 + step, h * HEAD_DIM:(h + 1) * HEAD_DIM] = acc / tot


def prompt_attention(q, k, v, seq):
    n_heads, m, _ = q.shape
    blk = MAX_DIL * ATTN_QB
    hg = 2
    assert seq % blk == 0 and n_heads % hg == 0 and m >= seq
    assert all(w == dil * ATTN_QB for w, dil in DILATED_BRANCHES)
    cur = pl.BlockSpec((hg, blk, HEAD_DIM), lambda i, c: (i, c, 0))
    prev = pl.BlockSpec((hg, blk, HEAD_DIM), lambda i, c: (i, jnp.maximum(c - 1, 0), 0))
    nbr = len(DILATED_BRANCHES)
    return pl.pallas_call(
        functools.partial(_prompt_attn_kernel, scale=1.0 / math.sqrt(HEAD_DIM)),
        out_shape=jax.ShapeDtypeStruct((seq, n_heads * HEAD_DIM), f32),
        grid=(n_heads // hg, seq // blk),
        in_specs=[cur, prev, cur, prev, cur],
        out_specs=pl.BlockSpec((blk, hg * HEAD_DIM), lambda i, c: (c, i)),
        scratch_shapes=[pltpu.VMEM((2 * blk, HEAD_DIM), f32), pltpu.VMEM((2 * blk, HEAD_DIM), f32),
                        pltpu.VMEM((nbr, blk, HEAD_DIM), f32), pltpu.VMEM((nbr, blk, LANES), f32)],
        compiler_params=_cparams(("parallel", "arbitrary"), 48),
        name="prompt_attn",
    )(q, k, k, v, v)


def _sample_key_positions(win_len, dec):
    half = MAX_DIL // 2
    ma, ra = np.meshgrid(np.arange(win_len // MAX_DIL), np.arange(half), indexing="ij")
    pos_a = (MAX_DIL * ma + ra).reshape(-1)
    mb0 = 3 * (win_len // MAX_DIL) // 4
    mb, rb = np.meshgrid(np.arange(mb0, win_len // MAX_DIL), np.arange(half, MAX_DIL), indexing="ij")
    pos_b = (MAX_DIL * mb + rb).reshape(-1)
    return np.concatenate([pos_a, pos_b, win_len + np.arange(dec)])


def _sample_bias(win_len, dec, n_heads):
    kpos = _sample_key_positions(win_len, dec)[:, None]
    qpos = PAST_LEN + (np.arange(LANES) % dec)[None, :]
    delta = qpos - kpos
    out = []
    for window, dil in DILATED_BRANCHES:
        ok = (delta >= 0) & (delta <= window) & (delta % dil == 0) & (kpos >= 0)
        out.append(np.where(ok, 0.0, NEG_BIG))
    out = np.stack(out).astype(np.float32)
    out[:, :, n_heads * dec:] = NEG_BIG
    return out


def _sample_attn_kernel(q_ref, kn_ref, vn_ref, ka_ref, kb_ref, va_ref, vb_ref, bias_ref, o_ref, *, scale):
    n_heads, dec, _ = q_ref.shape
    d = n_heads * HEAD_DIM
    nq = LANES

    def gather(ref):
        n = ref.shape[1] * ref.shape[2]
        return jnp.concatenate([ref[h].reshape(n, HEAD_DIM) for h in range(n_heads)], axis=-1).astype(bf16)

    def natural(ref):
        return jnp.concatenate([ref[h] for h in range(n_heads)], axis=-1)

    qt = jnp.concatenate([natural(q_ref)] * (nq // dec), axis=0)
    rowh = lax.broadcasted_iota(jnp.int32, (nq, d), 0) // dec
    colh = lax.broadcasted_iota(jnp.int32, (nq, d), 1) // HEAD_DIM
    qbd = jnp.where(rowh == colh, qt, 0.0).astype(bf16)
    nt = (((1,), (1,)), ((), ()))
    sa = lax.dot_general(gather(ka_ref), qbd, nt, preferred_element_type=f32)
    sb = lax.dot_general(gather(kb_ref), qbd, nt, preferred_element_type=f32)
    sn = lax.dot_general(natural(kn_ref).astype(bf16), qbd, nt, preferred_element_type=f32)
    na, nb = sa.shape[0], sb.shape[0]
    s = jnp.concatenate([sa, sb, sn], axis=0) * scale
    ps, ls, lses = [], [], []
    for g in range(len(DILATED_BRANCHES)):
        sg = s + bias_ref[g]
        m = jnp.max(sg, axis=0, keepdims=True)
        p = jnp.exp(sg - m)
        l = jnp.sum(p, axis=0, keepdims=True)
        ps.append(p)
        ls.append(l)
        lses.append(m + jnp.log(l))
    mx = functools.reduce(jnp.maximum, lses)
    ws = [jnp.exp(x - mx) for x in lses]
    tot = functools.reduce(lambda x, y: x + y, ws)
    pm = ps[0] * (ws[0] / (tot * ls[0]))
    for p, w, l in zip(ps[1:], ws[1:], ls[1:]):
        pm = pm + p * (w / (tot * l))
    pm = pm.astype(bf16)
    tn = (((0,), (0,)), ((), ()))
    o = lax.dot_general(pm[:na], gather(va_ref), tn, preferred_element_type=f32)
    o = o + lax.dot_general(pm[na:na + nb], gather(vb_ref), tn, preferred_element_type=f32)
    o = o + lax.dot_general(pm[na + nb:], natural(vn_ref).astype(bf16), tn, preferred_element_type=f32)
    for h in range(n_heads):
        hs = slice(h * HEAD_DIM, (h + 1) * HEAD_DIM)
        o_ref[:, hs] = o[h * dec:(h + 1) * dec, hs]


def sample_attention(q, k, v, cache_k, cache_v, row0):
    n_heads, m, _ = q.shape
    nbat, win_len = cache_k.shape[0], cache_k.shape[1]
    dec = (m - row0) // nbat
    assert win_len == PAST_LEN and dec == SUBLANES and row0 % dec == 0 and n_heads * dec <= LANES
    assert win_len % (4 * MAX_DIL) == 0 and DILATED_BRANCHES[1][0] <= win_len // 4
    half = MAX_DIL // 2
    mgrp = win_len // MAX_DIL
    d = n_heads * HEAD_DIM
    ck = cache_k.transpose(0, 2, 1, 3).reshape(nbat, n_heads, mgrp, MAX_DIL, HEAD_DIM)
    cv = cache_v.transpose(0, 2, 1, 3).reshape(nbat, n_heads, mgrp, MAX_DIL, HEAD_DIM)
    bias = jnp.asarray(_sample_bias(win_len, dec, n_heads))
    new = pl.BlockSpec((n_heads, dec, HEAD_DIM), lambda b: (0, row0 // dec + b, 0))
    part_a = pl.BlockSpec((None, n_heads, mgrp, half, HEAD_DIM), lambda b: (b, 0, 0, 0, 0))
    part_b = pl.BlockSpec((None, n_heads, mgrp // 4, half, HEAD_DIM), lambda b: (b, 0, 3, 1, 0))
    out = pl.pallas_call(
        functools.partial(_sample_attn_kernel, scale=1.0 / math.sqrt(HEAD_DIM)),
        out_shape=jax.ShapeDtypeStruct((nbat, dec, d), f32),
        grid=(nbat,),
        in_specs=[new, new, new, part_a, part_b, part_a, part_b, pl.BlockSpec(bias.shape, lambda b: (0, 0, 0))],
        out_specs=pl.BlockSpec((None, dec, d), lambda b: (b, 0, 0)),
        compiler_params=_cparams(("parallel",), 56),
        name="sample_attn",
    )(q, k, v, ck, ck, cv, cv, bias)
    return out.reshape(nbat * dec, d)


def _s5_discretise(a_re, a_im, dt):
    mag = jnp.exp(a_re * dt)
    lam_re = mag * jnp.cos(a_im * dt)
    lam_im = mag * jnp.sin(a_im * dt)
    num_re = lam_re - 1.0
    num_im = lam_im
    den = a_re * a_re + a_im * a_im
    f_re = (num_re * a_re + num_im * a_im) / den
    f_im = (num_im * a_re - num_re * a_im) / den
    return lam_re, lam_im, f_re, f_im


def _cmul(ar, ai, br, bi):
    return ar * br - ai * bi, ar * bi + ai * br


def _s5_operators_kernel(arc, aic, ldc, btr, bti, ccr, cci, arb, aib, ldb, cbr, cbi, ars, ais, lds,
                         w_ref, p_ref, q_ref, lr_ref, li_ref):
    t_len = SSM_T
    nt = (((1,), (1,)), ((), ()))
    hi = lax.Precision.HIGHEST
    gsz, ssz = SSM_GROUP, SSM_STATE

    def group_mask(shape, row_div, col_div):
        r = lax.broadcasted_iota(jnp.int32, shape, 0) // row_div
        c = lax.broadcasted_iota(jnp.int32, shape, 1) // col_div
        return r == c

    lr, li, fr, fi = _s5_discretise(arc[...], aic[...], jnp.exp(ldc[...]))
    er, ei = _cmul(fr, fi, btr[...], bti[...])
    c_re, c_im = ccr[...], cci[...]
    mask_w = group_mask((LANES, LANES), gsz, gsz)
    mask_p = group_mask((LANES, STATE_BLOCK), gsz, ssz)
    zero_blk = jnp.zeros((LANES, LANES), bf16)
    d_blocks, e_pows = [], []
    for n in range(t_len):
        e_pows.append((er, ei))
        dn = (lax.dot_general(er, c_re, nt, precision=hi, preferred_element_type=f32)
              - lax.dot_general(ei, c_im, nt, precision=hi, preferred_element_type=f32))
        d_blocks.append(jnp.where(mask_w, dn, 0.0).astype(bf16))
        er, ei = _cmul(lr, li, er, ei)
    for ti in range(t_len):
        for to in range(t_len):
            w_ref[ti * LANES:(ti + 1) * LANES, to * LANES:(to + 1) * LANES] = (
                d_blocks[to - ti] if to >= ti else zero_blk)
        pr, pi = e_pows[t_len - 1 - ti]
        p_ref[ti * LANES:(ti + 1) * LANES, 0:STATE_BLOCK] = jnp.where(
            mask_p, jnp.concatenate([pr] * GROUPS_PER_BLOCK, axis=1), 0.0).astype(bf16)
        p_ref[ti * LANES:(ti + 1) * LANES, STATE_BLOCK:] = jnp.where(
            mask_p, jnp.concatenate([pi] * GROUPS_PER_BLOCK, axis=1), 0.0).astype(bf16)
    lbr, lbi, _, _ = _s5_discretise(arb[...], aib[...], jnp.exp(ldb[...]))
    mask_q = group_mask((STATE_BLOCK, LANES), ssz, gsz)
    qr, qi = cbr[...], cbi[...]
    for t in range(t_len):
        qr, qi = _cmul(qr, qi, lbr, lbi)
        q_ref[0:STATE_BLOCK, t * LANES:(t + 1) * LANES] = jnp.where(
            mask_q, jnp.concatenate([qr] * GROUPS_PER_BLOCK, axis=0), 0.0).astype(bf16)
        q_ref[STATE_BLOCK:, t * LANES:(t + 1) * LANES] = jnp.where(
            mask_q, jnp.concatenate([-qi] * GROUPS_PER_BLOCK, axis=0), 0.0).astype(bf16)
    lsr, lsi, _, _ = _s5_discretise(ars[...], ais[...], jnp.exp(lds[...]))
    pr, pi = lsr, lsi
    for _ in range(t_len - 1):
        pr, pi = _cmul(pr, pi, lsr, lsi)
    lr_ref[...] = pr
    li_ref[...] = pi


def s5_chunk_operators(a_re, a_im, log_dt, b_re, b_im, c_re, c_im):
    g, p = a_re.shape
    ch = b_re.shape[2]
    j = GROUPS_PER_BLOCK
    nb = g // j
    assert p == SSM_STATE and ch == SSM_GROUP and g % j == 0
    t = SSM_T
    a_re, a_im, log_dt, b_re, b_im, c_re, c_im = (x.astype(f32) for x in (a_re, a_im, log_dt, b_re, b_im, c_re, c_im))
    by_row = lambda x: jnp.repeat(x.reshape(nb, j, p), ch, axis=1)
    by_col = lambda x: jnp.repeat(x.reshape(nb, j, p).transpose(0, 2, 1), ch, axis=2)
    ld = log_dt.reshape(nb, j)
    ld_row = jnp.repeat(ld, ch, axis=1)[:, :, None]
    ld_col = jnp.repeat(ld, ch, axis=1)[:, None, :]
    ld_flat = jnp.repeat(ld, p, axis=1)[:, None, :]
    bt = lambda x: x.reshape(nb, j, p, ch).transpose(0, 1, 3, 2).reshape(nb, j * ch, p)
    c_row = lambda x: x.reshape(nb, j * ch, p)
    c_col = lambda x: x.reshape(nb, j, ch, p).transpose(0, 3, 1, 2).reshape(nb, p, j * ch)
    flat = lambda x: x.reshape(nb, 1, j * p)
    ops = [by_row(a_re), by_row(a_im), ld_row, bt(b_re), bt(b_im), c_row(c_re), c_row(c_im),
           by_col(a_re), by_col(a_im), ld_col, c_col(c_re), c_col(c_im),
           flat(a_re), flat(a_im), ld_flat]
    blk = lambda arr: pl.BlockSpec((None,) + arr.shape[1:], lambda i: (i, 0, 0))
    shp = jax.ShapeDtypeStruct
    sb = STATE_BLOCK
    outs = [shp((nb, t * LANES, t * LANES), bf16), shp((nb, t * LANES, 2 * sb), bf16),
            shp((nb, 2 * sb, t * LANES), bf16), shp((nb, 1, sb), f32), shp((nb, 1, sb), f32)]
    return pl.pallas_call(
        _s5_operators_kernel,
        out_shape=outs,
        grid=(nb,),
        in_specs=[blk(x) for x in ops],
        out_specs=[blk(x) for x in outs],
        compiler_params=_cparams(("parallel",), 32),
        name="s5_operators",
    )(*ops)


def _s5_kernel(u_ref, w_ref, p_ref, q_ref, lr_ref, li_ref, dsk_ref, h0r_ref, h0i_ref,
               gl_ref, fpr_ref, fpi_ref, fsr_ref, fsi_ref, h_scr, *, segs_per_dot):
    seg, pitch = S5_SEG, S5_SEG_PITCH
    n_seg = u_ref.shape[0] // (SSM_T * seg)
    n_pseg = n_seg - 1
    nsl = 2 * STATE_BLOCK // LANES
    half = nsl // 2
    lr, li = lr_ref[...], li_ref[...]
    slab = lambda x, l: x[:, l * LANES:(l + 1) * LANES]

    def ucat(s0):
        r0, n = s0 * seg, segs_per_dot * seg
        return jnp.concatenate([u_ref[pl.ds(SSM_T * r0 + t, n, stride=SSM_T), :] for t in range(SSM_T)], axis=-1)

    def seg_rows(s):
        return slice(s * pitch, s * pitch + seg)

    for s0 in range(0, n_seg, segs_per_dot):
        res = jnp.dot(ucat(s0).astype(bf16), p_ref[...], preferred_element_type=f32)
        for s in range(segs_per_dot):
            for l in range(nsl):
                h_scr[l, seg_rows(s0 + s), :] = slab(res[s * seg:(s + 1) * seg], l)

    lam_r = [jnp.broadcast_to(slab(lr, l), (SUBLANES, LANES)) for l in range(half)]
    lam_i = [jnp.broadcast_to(slab(li, l), (SUBLANES, LANES)) for l in range(half)]

    def load(k):
        return [h_scr[l, pl.ds(k, SUBLANES, stride=pitch), :] for l in range(nsl)]

    def advance(c, x, mr, mi):
        re = [mr[l] * c[l] - mi[l] * c[l + half] + x[l] for l in range(half)]
        im = [mr[l] * c[l + half] + mi[l] * c[l] + x[l + half] for l in range(half)]
        return tuple(re + im)

    zeros = tuple(jnp.zeros((SUBLANES, LANES), f32) for _ in range(nsl))
    ends = lax.fori_loop(0, seg, lambda k, c: advance(c, load(k), lam_r, lam_i), zeros, unroll=8)

    pr, pi = [slab(lr, l) for l in range(half)], [slab(li, l) for l in range(half)]
    for _ in range(seg.bit_length() - 1):
        sq = [_cmul(a, b, a, b) for a, b in zip(pr, pi)]
        pr, pi = [x[0] for x in sq], [x[1] for x in sq]
    cur = tuple(jnp.zeros((1, LANES), f32) for _ in range(nsl))
    starts = [cur]
    for j in range(n_pseg):
        cur = advance(cur, [e[j:j + 1, :] for e in ends], pr, pi)
        starts.append(cur)
    fpr_ref[...] = jnp.concatenate(starts[n_pseg][:half], axis=-1)
    fpi_ref[...] = jnp.concatenate(starts[n_pseg][half:], axis=-1)
    init = tuple(jnp.concatenate([starts[j][l] for j in range(n_pseg)], axis=0) for l in range(nsl))

    def sweep(k, c):
        x = load(k)
        for l in range(nsl):
            h_scr[l, pl.ds(k, SUBLANES, stride=pitch), :] = c[l]
        return advance(c, x, lam_r, lam_i)

    lax.fori_loop(0, seg, sweep, init, unroll=8)

    h0r, h0i = h0r_ref[...], h0i_ref[...]
    last = seg_rows(n_pseg)
    inr = jnp.concatenate([h_scr[l, last, :] for l in range(half)], axis=-1)
    ini = jnp.concatenate([h_scr[l + half, last, :] for l in range(half)], axis=-1)
    fsr_ref[...] = lr * h0r - li * h0i + inr
    fsi_ref[...] = lr * h0i + li * h0r + ini
    for l in range(half):
        h_scr[l, last, :] = slab(h0r, l)
        h_scr[l + half, last, :] = slab(h0i, l)

    dsk = dsk_ref[...]
    for s0 in range(0, n_seg, segs_per_dot):
        u = ucat(s0)
        hp = jnp.concatenate([jnp.concatenate([h_scr[l, seg_rows(s0 + s), :] for l in range(nsl)], axis=-1)
                              for s in range(segs_per_dot)], axis=0)
        y = jnp.dot(u.astype(bf16), w_ref[...], preferred_element_type=f32)
        y = y + jnp.dot(hp.astype(bf16), q_ref[...], preferred_element_type=f32)
        gl = jax.nn.gelu(y + dsk * u)
        r0, n = s0 * seg, segs_per_dot * seg
        for t in range(SSM_T):
            gl_ref[pl.ds(SSM_T * r0 + t, n, stride=SSM_T), :] = gl[:, t * LANES:(t + 1) * LANES]


def s5_mixer(u, ops, d_skip, h0_re, h0_im, n_prompt):
    w, pm, qm, lr, li = ops
    m, dssm = u.shape
    nb = dssm // LANES
    t = SSM_T
    nbat = h0_re.shape[0]
    n_seg = m // (t * S5_SEG)
    segs_per_dot = 3
    assert n_prompt == SUBLANES * S5_SEG * t and nbat == S5_SEG and m == n_prompt + nbat * t
    assert n_seg % segs_per_dot == 0
    sb = STATE_BLOCK
    dsk = jnp.tile(d_skip.astype(f32).reshape(nb, 1, LANES), (1, 1, t))
    h0r = h0_re.astype(f32).reshape(nbat, nb * sb)
    h0i = h0_im.astype(f32).reshape(nbat, nb * sb)
    blk = lambda shape: pl.BlockSpec((None,) + shape, lambda i: (i, 0, 0))
    ublk = pl.BlockSpec((m, LANES), lambda i: (0, i))
    hblk = pl.BlockSpec((nbat, sb), lambda i: (0, i))
    fblk = pl.BlockSpec((1, sb), lambda i: (0, i))
    shp = jax.ShapeDtypeStruct
    return pl.pallas_call(
        functools.partial(_s5_kernel, segs_per_dot=segs_per_dot),
        out_shape=[shp((m, dssm), f32), shp((1, nb * sb), f32), shp((1, nb * sb), f32),
                   shp((nbat, nb * sb), f32), shp((nbat, nb * sb), f32)],
        grid=(nb,),
        in_specs=[ublk, blk((t * LANES, t * LANES)), blk((t * LANES, 2 * sb)), blk((2 * sb, t * LANES)),
                  blk((1, sb)), blk((1, sb)), blk((1, t * LANES)), hblk, hblk],
        out_specs=[ublk, fblk, fblk, hblk, hblk],
        scratch_shapes=[pltpu.VMEM((2 * sb // LANES, n_seg * S5_SEG_PITCH, LANES), f32)],
        compiler_params=_cparams(("parallel",), 56),
        name="s5_mixer",
    )(u, w, pm, qm, lr, li, dsk, h0r, h0i)


def _cross_attn_cache_kernel(q_ref, k_ref, v_ref, o_ref, *, scale):
    gb, n, nh, hd = k_ref.shape
    dec = q_ref.shape[1]
    rowh = lax.broadcasted_iota(jnp.int32, (n * nh, nh * dec), 0) % nh
    colh = lax.broadcasted_iota(jnp.int32, (n * nh, nh * dec), 1) // dec
    same = rowh == colh
    for g in range(gb):
        kf = k_ref[g].reshape(n * nh, hd).astype(bf16)
        vf = v_ref[g].reshape(n * nh, hd).astype(bf16)
        q = q_ref[g]
        qs = jnp.concatenate([q[:, h * hd:(h + 1) * hd] for h in range(nh)], axis=0).astype(bf16)
        s = lax.dot_general(kf, qs, (((1,), (1,)), ((), ())), preferred_element_type=f32) * scale
        s = jnp.where(same, s, NEG_BIG)
        m = jnp.max(s, axis=0, keepdims=True)
        p = jnp.exp(s - m)
        p = p / jnp.sum(p, axis=0, keepdims=True)
        o = lax.dot_general(p.astype(bf16), vf, (((0,), (0,)), ((), ())), preferred_element_type=f32)
        for h in range(nh):
            o_ref[g, :, h * hd:(h + 1) * hd] = o[h * dec:(h + 1) * dec, :].astype(o_ref.dtype)


def _cross_attn_kernel(q_ref, k_ref, v_ref, o_ref, *, scale):
    q = q_ref[...].astype(bf16)
    for h in range(MEM_HEADS):
        hs = slice(h * MEM_HEAD_DIM, (h + 1) * MEM_HEAD_DIM)
        k = k_ref[:, :, hs].astype(bf16)
        v = v_ref[:, :, hs].astype(bf16)
        s = jnp.einsum("gqd,gkd->gqk", q[:, :, hs], k, preferred_element_type=f32) * scale
        m = jnp.max(s, axis=-1, keepdims=True)
        p = jnp.exp(s - m)
        p = p / jnp.sum(p, axis=-1, keepdims=True)
        o = jnp.einsum("gqk,gkd->gqd", p.astype(bf16), v, preferred_element_type=f32)
        o_ref[:, :, hs] = o.astype(o_ref.dtype)


def cross_attention(q3, k, v, nq, gb, tq, q_block0, name):
    g = k.shape[0]
    e = q3.shape[2]
    kblock = (gb,) + k.shape[1:]
    kmap = (lambda i, j: (i, 0, 0)) if k.ndim == 3 else (lambda i, j: (i, 0, 0, 0))
    body = _cross_attn_kernel if k.ndim == 3 else _cross_attn_cache_kernel
    return pl.pallas_call(
        functools.partial(body, scale=1.0 / math.sqrt(MEM_HEAD_DIM)),
        out_shape=jax.ShapeDtypeStruct((g, nq, e), bf16),
        grid=(g // gb, nq // tq),
        in_specs=[pl.BlockSpec((gb, tq, e), lambda i, j: (q_block0 + i, j, 0)),
                  pl.BlockSpec(kblock, kmap), pl.BlockSpec(kblock, kmap)],
        out_specs=pl.BlockSpec((gb, tq, e), lambda i, j: (i, j, 0)),
        compiler_params=_cparams(("parallel", "parallel"), 48),
        name=name,
    )(q3, k, v)


def _rotary_tables(pos):
    inv = 1.0 / (ROPE_THETA ** (jnp.arange(0, HEAD_DIM, 2, dtype=f32) / HEAD_DIM))
    ang = pos.astype(f32)[:, None] * inv[None, :]
    cos, sin = jnp.cos(ang), jnp.sin(ang)
    return jnp.concatenate([cos, cos], axis=-1), jnp.concatenate([-sin, sin], axis=-1)


def kernel(x_prompt, x_sample, cache_win_k, cache_win_v, state_ssm_re, state_ssm_im, cache_mem_k, cache_mem_v,
           mem_prompt, g_mix, w_in, a_re, a_im, log_dt, b_re, b_im, c_re, c_im, d_skip, w_glu, b_glu,
           g_attn_out, g_ssm_out, w_out, g_cross, g_mem, w_mq, w_mk, w_mv, w_mo, g_ffn, w_up, w_down, g_final):
    bp, sp, dm = x_prompt.shape
    nbat, dec, _ = x_sample.shape
    assert bp == 1
    ns = nbat * dec
    n_groups = a_re.shape[0]
    d_ssm = n_groups * SSM_GROUP
    d_attn = dm - d_ssm
    n_heads = d_attn // HEAD_DIM
    e_mem = MEM_HEADS * MEM_HEAD_DIM
    n_mem = mem_prompt.shape[1]
    keep = min(PAST_LEN, sp)

    xp, xs = x_prompt.reshape(sp, dm), x_sample.reshape(ns, dm)
    mtot = sp + ns
    pos = jnp.concatenate([jnp.arange(sp, dtype=jnp.int32), PAST_LEN + jnp.tile(jnp.arange(dec, dtype=jnp.int32), nbat)])
    cos2, sin2 = _rotary_tables(pos)
    wb = lambda w_: w_.astype(bf16)

    h = rmsnorm2(xp, xs, g_mix, bf16)
    rot =((cos2, "rowtab"), (sin2, "rowtab"))
    w_in_b = wb(w_in)
    q = matmul(h, w_in_b, _ep_rotary, rot, head_major=True, cols=(0, d_attn), name="proj_q")
    k = matmul(h, w_in_b, _ep_rotary, rot, head_major=True, cols=(d_attn, d_attn), name="proj_k")
    v = matmul(h, w_in_b, head_major=True, cols=(2 * d_attn, d_attn), name="proj_v")
    u = matmul(h, w_in_b, cols=(3 * d_attn, d_ssm), name="proj_u")

    attn_p = prompt_attention(q, k, v, sp)
    attn_s = sample_attention(q, k, v, cache_win_k, cache_win_v, sp)
    na = rmsnorm2(attn_p, attn_s, g_attn_out, bf16)

    ops = s5_chunk_operators(a_re, a_im, log_dt, b_re, b_im, c_re, c_im)
    gl, fpr, fpi, fsr, fsi = s5_mixer(u, ops, d_skip, state_ssm_re, state_ssm_im, sp)
    ny = glu_norm(gl, wb(w_glu), b_glu, g_ssm_out)

    x1 = out_projection(na, ny, wb(w_out[:d_attn]), wb(w_out[d_attn:]), xp, xs)

    mem_n = rmsnorm(mem_prompt.reshape(n_mem, dm), g_mem, bf16)
    mem_k_p = matmul(mem_n, wb(w_mk), name="mem_k")
    mem_v_p = matmul(mem_n, wb(w_mv), name="mem_v")
    qm = norm_projection(x1, g_cross, wb(w_mq))
    o_p = cross_attention(qm.reshape(1, mtot, e_mem), mem_k_p.reshape(1, n_mem, e_mem), mem_v_p.reshape(1, n_mem, e_mem),
                          sp, 1, 512, 0, "cross_attn_prompt")
    o_s = cross_attention(qm.reshape(mtot // dec, dec, e_mem), cache_mem_k, cache_mem_v,
                          dec, 8, dec, sp // dec // 8, "cross_attn_sample")
    o_c = jnp.concatenate([o_p.reshape(sp, e_mem), o_s.reshape(ns, e_mem)], axis=0)
    x2, hf = projection_norm(o_c, wb(w_mo), x1, g_ffn)

    hid = matmul(hf, w_up, _ep_relu2, out_dtype=bf16, tm_pref=1536, tn_cap=512, name="ffn_up")
    x3 = matmul(hid, wb(w_down), _ep_residual, ((x2, "tile"),), tm_pref=1536, name="ffn_down")
    y_p = rmsnorm(x3, g_final, f32, 0, sp)
    y_s = rmsnorm(x3, g_final, f32, sp, ns)

    hd = (n_heads, HEAD_DIM)
    natural = lambda t, r0, r1: t[:, r0:r1].transpose(1, 0, 2)
    return (y_p.reshape(1, sp, dm), y_s.reshape(nbat, dec, dm),
            natural(k, sp - keep, sp).reshape(1, keep, *hd), natural(v, sp - keep, sp).reshape(1, keep, *hd),
            fpr.reshape(1, n_groups, SSM_STATE), fpi.reshape(1, n_groups, SSM_STATE),
            mem_k_p.reshape(1, n_mem, MEM_HEADS, MEM_HEAD_DIM), mem_v_p.reshape(1, n_mem, MEM_HEADS, MEM_HEAD_DIM),
            natural(k, sp, mtot).reshape(nbat, dec, *hd), natural(v, sp, mtot).reshape(nbat, dec, *hd),
            fsr.reshape(nbat, n_groups, SSM_STATE), fsi.reshape(nbat, n_groups, SSM_STATE))
```
